```python
import numpy as np
import jax
import jax.numpy as jnp
from jax import lax

D_MODEL = 1024
BATCH = 8
SEQ = 4096
DEPTH = 2

GRID_W = 64
CTX_LEN = 256
MLA_HEADS = 8
MLA_Q_LORA = 384
MLA_KV_LORA = 256
MLA_NOPE = 64
MLA_ROPE = 32
MLA_V = 64
MLA_SCALE = (MLA_NOPE + MLA_ROPE) ** -0.5
ROPE_BASE = 10000.0
Q_BLOCK = 128
HG_HEADS = 4
HG_DK = 128
HG_DV = 128
HG_CHUNK = 32
HG_K = HG_HEADS * HG_DK
HG_V = HG_HEADS * HG_DV
IN_SPLITS = (MLA_Q_LORA, MLA_KV_LORA, MLA_ROPE, HG_K, HG_K, HG_K, HG_V, HG_V)
IN_COLS = sum(IN_SPLITS)
MIX_WIDTH = MLA_HEADS * MLA_V + HG_V
CONV_WIDTH = 31
CONF_DIM = D_MODEL
N_EXPERTS = 32
TOP_K = 4
D_EXPERT = D_MODEL
SWIGLU_LIMIT = 7.0
SWIGLU_ALPHA = 1.702
MOE_BLOCK = 128
LN_EPS = 1e-5
RMS_EPS = 1e-6
DEEPNORM_ALPHA = (2.0 * DEPTH) ** 0.25
DEEPNORM_BETA = (8.0 * DEPTH) ** -0.25
N_EVEN = (DEPTH + 1) // 2
N_ODD = DEPTH // 2

kernel_name = 'hybrid_mla_hgrn2_conformer_moe'


def _rmsnorm(x, g):
    xf = x.astype(jnp.float32)
    y = xf * lax.rsqrt(jnp.mean(xf * xf, axis=-1, keepdims=True) + RMS_EPS)
    return (y * g.astype(jnp.float32)).astype(x.dtype)


def _layernorm(x, g, b):
    xf = x.astype(jnp.float32)
    xc = xf - jnp.mean(xf, axis=-1, keepdims=True)
    var = jnp.mean(xc * xc, axis=-1, keepdims=True)
    return (xc * lax.rsqrt(var + LN_EPS) * g.astype(jnp.float32) + b.astype(jnp.float32)).astype(x.dtype)


def _axial_rope(rows):
    pos_r = jnp.repeat(jnp.arange(rows, dtype=jnp.float32), GRID_W)
    pos_c = jnp.tile(jnp.arange(GRID_W, dtype=jnp.float32), rows)
    half = MLA_ROPE // 2
    inv = jnp.power(ROPE_BASE, -jnp.arange(0, half, 2, dtype=jnp.float32) / half)
    ang = jnp.concatenate([pos_r[:, None] * inv, pos_c[:, None] * inv], axis=-1)
    return jnp.cos(ang), jnp.sin(ang)


def _rope(x, cos, sin):
    x1, x2 = x[..., 0::2], x[..., 1::2]
    cs, sn = cos.astype(x.dtype), sin.astype(x.dtype)
    return jnp.stack([x1 * cs - x2 * sn, x1 * sn + x2 * cs], axis=-1).reshape(x.shape)


def _mla_heads(cq, ckv, kr, q_norm, kv_norm, w_uq, w_uk, w_uv):
    B, T, _ = cq.shape
    q = (_rmsnorm(cq, q_norm) @ w_uq).reshape(B, T, MLA_HEADS, MLA_NOPE + MLA_ROPE)
    ckv = _rmsnorm(ckv, kv_norm)
    k_nope = (ckv @ w_uk).reshape(B, T, MLA_HEADS, MLA_NOPE)
    v = (ckv @ w_uv).reshape(B, T, MLA_HEADS, MLA_V)
    return q[..., :MLA_NOPE], q[..., MLA_NOPE:], k_nope, kr, v


def _mla_attend(qn, qr, kn, kr, v):
    s = jnp.einsum('bqhd,bkhd->bhqk', qn, kn) + jnp.einsum('bqhr,bkr->bhqk', qr, kr)
    p = jax.nn.softmax(s.astype(jnp.float32) * MLA_SCALE, axis=-1).astype(v.dtype)
    return jnp.einsum('bhqk,bkhd->bqhd', p, v)


def _mla_latent(qn, qr, kn, kr, v):
    B, n, H, _ = qn.shape
    nb = n // Q_BLOCK

    def blocks(a):
        return jnp.moveaxis(a.reshape(B, nb, Q_BLOCK, *a.shape[2:]), 1, 0)

    o = lax.map(lambda qs: _mla_attend(qs[0], qs[1], kn, kr, v), (blocks(qn), blocks(qr)))
    return jnp.moveaxis(o, 0, 1).reshape(B, n, H * MLA_V)


def _hgrn_chunk_scan(q, k, v, log_f, s0):
    B, T, H, _ = q.shape
    nc = T // HG_CHUNK

    def chunks(a):
        return a.reshape(B, nc, HG_CHUNK, H, a.shape[-1]).transpose(1, 0, 3, 2, 4)

    tri = jnp.tril(jnp.ones((HG_CHUNK, HG_CHUNK), dtype=bool))

    def step(S, xs):
        qc, kc, vc, gc = xs
        b = jnp.cumsum(gc, axis=2)
        b_end = b[:, :, -1:, :]
        q_in = qc * jnp.exp(b)
        a = jnp.einsum('bhtk,bhsk->bhts', q_in, kc * jnp.exp(-b))
        a = jnp.where(tri, a, 0.0)
        o = jnp.einsum('bhts,bhsv->bhtv', a, vc) + jnp.einsum('bhtk,bhkv->bhtv', q_in, S)
        S = jnp.exp(b_end[:, :, 0, :])[..., None] * S + jnp.einsum('bhsk,bhsv->bhkv', kc * jnp.exp(b_end - b), vc)
        return S, o

    S, o = lax.scan(step, s0, (chunks(q), chunks(k), chunks(v), chunks(log_f)))
    return o.transpose(1, 0, 3, 2, 4).reshape(B, T, H, v.shape[-1]), S


def _hgrn_bidir(parts_c, parts_l, lb, hg_norm, need_ctx):
    lbh = lb.reshape(HG_HEADS, HG_DK)

    def heads(a, d):
        B, T, _ = a.shape
        return a.reshape(B, T, HG_HEADS, d).astype(jnp.float32)

    def gate(hf):
        f = lbh + (1.0 - lbh) * jax.nn.sigmoid(heads(hf, HG_DK))
        return 1.0 - f, jnp.log(f)

    q_c, v_c = heads(parts_c[0], HG_DK), heads(parts_c[3], HG_DV)
    q_l, v_l = heads(parts_l[0], HG_DK), heads(parts_l[3], HG_DV)
    B = q_l.shape[0]
    o_c = jnp.zeros_like(v_c)
    o_l = jnp.zeros_like(v_l)
    for fi, reverse in ((1, False), (2, True)):
        k_c, g_c = gate(parts_c[fi])
        k_l, g_l = gate(parts_l[fi])
        seq_c, seq_l = (q_c, k_c, v_c, g_c), (q_l, k_l, v_l, g_l)
        if reverse:
            seq_c = tuple(jnp.flip(a, 1) for a in seq_c)
            seq_l = tuple(jnp.flip(a, 1) for a in seq_l)
        s0 = jnp.zeros((B, HG_HEADS, HG_DK, HG_DV), jnp.float32)
        oc, s_ctx = _hgrn_chunk_scan(*seq_c, s0)
        ol, _ = _hgrn_chunk_scan(*seq_l, s_ctx)
        if reverse:
            oc, ol = jnp.flip(oc, 1), jnp.flip(ol, 1)
        o_c = o_c + oc
        o_l = o_l + ol

    def readout(o, g):
        B2, T = o.shape[:2]
        y = _rmsnorm(o, hg_norm) * jax.nn.silu(heads(g, HG_DV))
        return y.reshape(B2, T, HG_V).astype(g.dtype)

    y_c = readout(o_c, parts_c[4]) if need_ctx else None
    return y_c, readout(o_l, parts_l[4])


def _mla_hgrn_mixer(a_ctx, a_lat, w_in, q_norm, kv_norm, w_uq, w_uk, w_uv, lb, hg_norm, w_out, cos, sin, need_ctx):
    splits = np.cumsum(IN_SPLITS)[:-1].tolist()
    zc = jnp.split(a_ctx @ w_in, splits, axis=-1)
    zl = jnp.split(a_lat @ w_in, splits, axis=-1)
    qn_c, qr_c, kn_c, kr_c, v_c = _mla_heads(zc[0], zc[1], zc[2], q_norm, kv_norm, w_uq, w_uk, w_uv)
    qn_l, qr_l, kn_l, kr_l, v_l = _mla_heads(zl[0], zl[1], zl[2], q_norm, kv_norm, w_uq, w_uk, w_uv)
    qr_l = _rope(qr_l, cos[:, None, :], sin[:, None, :])
    kr_l = _rope(kr_l, cos, sin)
    att_l = _mla_latent(qn_l, qr_l,
                        jnp.concatenate([kn_c, kn_l], axis=1),
                        jnp.concatenate([kr_c, kr_l], axis=1),
                        jnp.concatenate([v_c, v_l], axis=1))
    hg_c, hg_l = _hgrn_bidir(zc[3:], zl[3:], lb, hg_norm, need_ctx)
    y_lat = jnp.concatenate([att_l, hg_l], axis=-1) @ w_out
    y_ctx = None
    if need_ctx:
        B, Tc = a_ctx.shape[:2]
        att_c = _mla_attend(qn_c, qr_c, kn_c, kr_c, v_c).reshape(B, Tc, MLA_HEADS * MLA_V)
        y_ctx = jnp.concatenate([att_c, hg_c], axis=-1) @ w_out
    return y_ctx, y_lat


def _conformer_conv(h, pw1_w, pw1_b, dw_w, dw_b, ln_g, ln_b, pw2_w, pw2_b):
    a = h @ pw1_w + pw1_b
    u = a[..., :CONF_DIM] * jax.nn.sigmoid(a[..., CONF_DIM:])
    u = lax.conv_general_dilated(u, dw_w[:, None, :].astype(u.dtype), window_strides=(1,),
                                 padding=[(CONV_WIDTH // 2, CONV_WIDTH // 2)],
                                 dimension_numbers=('NWC', 'WIO', 'NWC'),
                                 feature_group_count=CONF_DIM) + dw_b
    u = jax.nn.silu(_layernorm(u, ln_g, ln_b))
    return u @ pw2_w + pw2_b


def _clamped_swiglu(gu):
    gate = jnp.minimum(gu[..., 0::2], SWIGLU_LIMIT)
    up = jnp.clip(gu[..., 1::2], -SWIGLU_LIMIT, SWIGLU_LIMIT)
    return (up + 1.0) * gate * jax.nn.sigmoid(SWIGLU_ALPHA * gate)


def _moe(h, router_w, router_b, w_gu, b_gu, w_down, b_down):
    N, D = h.shape
    logits = (h @ router_w + router_b).astype(jnp.float32)
    top_v, top_i = lax.top_k(logits, TOP_K)
    gates = jax.nn.softmax(top_v, axis=-1).astype(h.dtype)
    nk = N * TOP_K
    flat_e = top_i.reshape(-1)
    flat_w = gates.reshape(-1)
    flat_tok = jnp.arange(nk, dtype=jnp.int32) // TOP_K
    order = jnp.argsort(flat_e)
    se = flat_e[order]
    counts = jnp.zeros((N_EXPERTS,), jnp.int32).at[flat_e].add(1)
    starts = jnp.cumsum(counts) - counts
    pcounts = (counts + MOE_BLOCK - 1) // MOE_BLOCK * MOE_BLOCK
    pends = jnp.cumsum(pcounts)
    pstarts = pends - pcounts
    dest = pstarts[se] + jnp.arange(nk, dtype=jnp.int32) - starts[se]
    n_blocks = -(-(nk + N_EXPERTS * (MOE_BLOCK - 1)) // MOE_BLOCK)
    P = n_blocks * MOE_BLOCK
    row_tok = jnp.zeros((P,), jnp.int32).at[dest].set(flat_tok[order])
    row_w = jnp.zeros((P,), h.dtype).at[dest].set(flat_w[order])
    block_e = jnp.minimum(jnp.searchsorted(pends, jnp.arange(n_blocks, dtype=jnp.int32) * MOE_BLOCK, side='right'),
                          N_EXPERTS - 1)

    def step(out, xs):
        tok, wt, e = xs
        gu = h[tok] @ w_gu[e] + b_gu[e]
        y = _clamped_swiglu(gu) @ w_down[e] + b_down[e]
        return out.at[tok].add((y * wt[:, None]).astype(jnp.float32)), None

    out, _ = lax.scan(step, jnp.zeros((N, D), jnp.float32),
                      (row_tok.reshape(n_blocks, MOE_BLOCK), row_w.reshape(n_blocks, MOE_BLOCK), block_e))
    return out.astype(h.dtype)


def setup_inputs(seed: int = 0) -> dict:
    key = jax.random.key(seed)
    ks = iter(jax.random.split(key, 40))
    D = D_MODEL

    def nrm(shape, scale):
        return jax.random.normal(next(ks), shape, jnp.float32) * scale

    def gain(shape):
        return 1.0 + nrm(shape, 0.02)

    return {
        'x': nrm((BATCH, SEQ, D), 1.0),
        'c': nrm((BATCH, D), 1.0),
        'ctx': nrm((BATCH, CTX_LEN, D), 1.0),
        'c_ctx': nrm((D,), 1.0),
        'mod_w': nrm((DEPTH, D, 6 * D), 0.5 * D ** -0.5),
        'mod_b': nrm((DEPTH, 6 * D), 0.02),
        'ln1_g': gain((DEPTH, D)),
        'ln1_b': nrm((DEPTH, D), 0.02),
        'ln2_g': gain((DEPTH, D)),
        'ln2_b': nrm((DEPTH, D), 0.02),
        'mix_w_in': nrm((N_EVEN, D, IN_COLS), D ** -0.5),
        'mla_q_norm': gain((N_EVEN, MLA_Q_LORA)),
        'mla_kv_norm': gain((N_EVEN, MLA_KV_LORA)),
        'mla_w_uq': nrm((N_EVEN, MLA_Q_LORA, MLA_HEADS * (MLA_NOPE + MLA_ROPE)), MLA_Q_LORA ** -0.5),
        'mla_w_uk': nrm((N_EVEN, MLA_KV_LORA, MLA_HEADS * MLA_NOPE), MLA_KV_LORA ** -0.5),
        'mla_w_uv': nrm((N_EVEN, MLA_KV_LORA, MLA_HEADS * MLA_V), MLA_KV_LORA ** -0.5),
        'hg_lb_logits': nrm((DEPTH + 1, HG_K), 0.1),
        'hg_norm': gain((N_EVEN, HG_DV)),
        'mix_w_out': nrm((N_EVEN, MIX_WIDTH, D), DEEPNORM_BETA * MIX_WIDTH ** -0.5),
        'conv_pw1_w': nrm((N_ODD, D, 2 * CONF_DIM), D ** -0.5),
        'conv_pw1_b': nrm((N_ODD, 2 * CONF_DIM), 0.02),
        'conv_dw_w': nrm((N_ODD, CONV_WIDTH, CONF_DIM), CONV_WIDTH ** -0.5),
        'conv_dw_b': nrm((N_ODD, CONF_DIM), 0.02),
        'conv_ln_g': gain((N_ODD, CONF_DIM)),
        'conv_ln_b': nrm((N_ODD, CONF_DIM), 0.02),
        'conv_pw2_w': nrm((N_ODD, CONF_DIM, D), DEEPNORM_BETA * CONF_DIM ** -0.5),
        'conv_pw2_b': nrm((N_ODD, D), 0.02),
        'router_w': nrm((DEPTH, D, N_EXPERTS), D ** -0.5),
        'router_b': nrm((DEPTH, N_EXPERTS), 0.01),
        'exp_w_gu': nrm((DEPTH, N_EXPERTS, D, 2 * D_EXPERT), D ** -0.5),
        'exp_b_gu': nrm((DEPTH, N_EXPERTS, 2 * D_EXPERT), 0.02),
        'exp_w_down': nrm((DEPTH, N_EXPERTS, D_EXPERT, D), DEEPNORM_BETA * D_EXPERT ** -0.5),
        'exp_b_down': nrm((DEPTH, N_EXPERTS, D), 0.02),
    }


def reference(x, c, ctx, c_ctx, mod_w, mod_b, ln1_g, ln1_b, ln2_g, ln2_b,
              mix_w_in, mla_q_norm, mla_kv_norm, mla_w_uq, mla_w_uk, mla_w_uv,
              hg_lb_logits, hg_norm, mix_w_out,
              conv_pw1_w, conv_pw1_b, conv_dw_w, conv_dw_b, conv_ln_g, conv_ln_b,
              conv_pw2_w, conv_pw2_b,
              router_w, router_b, exp_w_gu, exp_b_gu, exp_w_down, exp_b_down):
    B, n, D = x.shape
    rows = n // GRID_W
    cos, sin = _axial_rope(rows)
    lb_all = jnp.cumsum(jax.nn.softmax(hg_lb_logits.astype(jnp.float32), axis=0), axis=0)
    h_lat, h_ctx = x, ctx
    for l in range(DEPTH):
        last = l == DEPTH - 1
        j = l // 2
        m_lat = jnp.split((jax.nn.silu(c) @ mod_w[l] + mod_b[l])[:, None, :], 6, axis=-1)
        m_ctx = jnp.split(jax.nn.silu(c_ctx) @ mod_w[l] + mod_b[l], 6, axis=-1)
        a_lat = h_lat * (1.0 + m_lat[1]) + m_lat[0]
        a_ctx = h_ctx * (1.0 + m_ctx[1]) + m_ctx[0]
        if l % 2 == 0:
            y_ctx, y_lat = _mla_hgrn_mixer(a_ctx, a_lat, mix_w_in[j], mla_q_norm[j], mla_kv_norm[j],
                                           mla_w_uq[j], mla_w_uk[j], mla_w_uv[j], lb_all[l], hg_norm[j],
                                           mix_w_out[j], cos, sin, not last)
        else:
            conv_p = (conv_pw1_w[j], conv_pw1_b[j], conv_dw_w[j], conv_dw_b[j],
                      conv_ln_g[j], conv_ln_b[j], conv_pw2_w[j], conv_pw2_b[j])
            y_lat = _conformer_conv(a_lat, *conv_p)
            y_ctx = None if last else _conformer_conv(a_ctx, *conv_p)
        h_lat = _layernorm(DEEPNORM_ALPHA * h_lat + m_lat[2] * y_lat, ln1_g[l], ln1_b[l])
        moe_p = (router_w[l], router_b[l], exp_w_gu[l], exp_b_gu[l], exp_w_down[l], exp_b_down[l])
        b_lat = h_lat * (1.0 + m_lat[4]) + m_lat[3]
        if last:
            f_lat = _moe(b_lat.reshape(B * n, D), *moe_p).reshape(B, n, D)
        else:
            h_ctx = _layernorm(DEEPNORM_ALPHA * h_ctx + m_ctx[2] * y_ctx, ln1_g[l], ln1_b[l])
            b_ctx = h_ctx * (1.0 + m_ctx[4]) + m_ctx[3]
            Tc = h_ctx.shape[1]
            f_all = _moe(jnp.concatenate([b_ctx.reshape(B * Tc, D), b_lat.reshape(B * n, D)], axis=0), *moe_p)
            f_ctx = f_all[:B * Tc].reshape(B, Tc, D)
            f_lat = f_all[B * Tc:].reshape(B, n, D)
            h_ctx = _layernorm(DEEPNORM_ALPHA * h_ctx + m_ctx[5] * f_ctx, ln2_g[l], ln2_b[l])
        h_lat = _layernorm(DEEPNORM_ALPHA * h_lat + m_lat[5] * f_lat, ln2_g[l], ln2_b[l])
    return h_lat
```

```python
import functools

import numpy as np
import jax
import jax.numpy as jnp
from jax import lax
from jax.experimental import pallas as pl
from jax.experimental.pallas import tpu as pltpu

F32 = jnp.float32
BF16 = jnp.bfloat16

D_MODEL = 1024
DEPTH = 2
GRID_W = 64
MLA_HEADS = 8
MLA_Q_LORA = 384
MLA_KV_LORA = 256
MLA_NOPE = 64
MLA_ROPE = 32
MLA_V = 64
MLA_SCALE = (MLA_NOPE + MLA_ROPE) ** -0.5
ROPE_BASE = 10000.0
HG_HEADS = 4
HG_DK = 128
HG_CHUNK = 32
HG_K = HG_HEADS * HG_DK
CONV_WIDTH = 31
N_EXPERTS = 32
TOP_K = 4
SWIGLU_LIMIT = 7.0
SWIGLU_ALPHA = 1.702
LN_EPS = 1e-5
RMS_EPS = 1e-6
DEEPNORM_ALPHA = (2.0 * DEPTH) ** 0.25

LANES = 128
ROW_TILE = 256
HEAD_BLOCK = 128
KEY_CHUNK = 1024
MOE_TILE = 256
CONV_HALO = 16
IN_WIDTH = 3456
HG_PARTS = 5 * HG_K
VMEM_LIMIT = 56 * 1024 * 1024


def _params(*sem):
    return pltpu.CompilerParams(dimension_semantics=sem, vmem_limit_bytes=VMEM_LIMIT)


def _sigmoid(x):
    return 1.0 / (1.0 + jnp.exp(-x))


def _layernorm(x, g, b):
    xc = x - jnp.mean(x, axis=-1, keepdims=True)
    var = jnp.mean(xc * xc, axis=-1, keepdims=True)
    return xc * lax.rsqrt(var + LN_EPS) * g + b


def _rms(x, g):
    return x * lax.rsqrt(jnp.mean(x * x, axis=-1, keepdims=True) + RMS_EPS) * g


def _mod_kernel(c_ref, w_ref, b_ref, o_ref):
    c = c_ref[...]
    s = (c * _sigmoid(c)).astype(BF16)
    o_ref[...] = jnp.dot(s, w_ref[...].astype(BF16), preferred_element_type=F32) + b_ref[...]


def _modulation(cond, mod_w, mod_b):
    L, D, W = mod_w.shape
    R = cond.shape[0]
    return pl.pallas_call(
        _mod_kernel,
        grid=(L, W // D),
        in_specs=[pl.BlockSpec((R, D), lambda l, n: (0, 0)),
                  pl.BlockSpec((None, D, D), lambda l, n: (l, 0, n)),
                  pl.BlockSpec((None, 1, D), lambda l, n: (l, 0, n))],
        out_specs=pl.BlockSpec((None, R, D), lambda l, n: (l, 0, n)),
        out_shape=jax.ShapeDtypeStruct((L, R, W), F32),
        compiler_params=_params("arbitrary", "arbitrary"),
        name="adaln_mod",
    )(cond, mod_w, mod_b.reshape(L, 1, W))


def _inproj_kernel(h_ref, mod_ref, win_ref, qn_ref, kvn_ref, wq_ref, wkv_ref, c_ref, s_ref,
                   q_out, k_out, v_out, hg_out):
    m = mod_ref[...]
    a = (h_ref[...] * (1.0 + m[1:2]) + m[0:1]).astype(BF16)
    z = jnp.dot(a, win_ref[...], preferred_element_type=F32)
    cos = c_ref[...]
    sin = s_ref[...]
    cos_h = jnp.concatenate([cos] * MLA_HEADS, axis=1)
    sin_h = jnp.concatenate([sin] * MLA_HEADS, axis=1)
    width = MLA_HEADS * HEAD_BLOCK

    cqn = _rms(z[:, :MLA_Q_LORA], qn_ref[...]).astype(BF16)
    qq = jnp.dot(cqn, wq_ref[...], preferred_element_type=F32)
    q_out[...] = ((qq[:, :width] * cos_h + qq[:, width:] * sin_h) * MLA_SCALE).astype(BF16)

    c0 = MLA_Q_LORA + MLA_KV_LORA
    ckvn = _rms(z[:, MLA_Q_LORA:c0], kvn_ref[...]).astype(BF16)
    kv = jnp.dot(ckvn, wkv_ref[...], preferred_element_type=F32)
    r0 = c0 + HG_PARTS
    kr = z[:, r0:r0 + HEAD_BLOCK] * cos + z[:, r0 + HEAD_BLOCK:r0 + 2 * HEAD_BLOCK] * sin
    k_out[...] = (kv[:, :width] + jnp.concatenate([kr] * MLA_HEADS, axis=1)).astype(BF16)
    v_out[...] = kv[:, width:].astype(BF16)
    hg_out[...] = z[:, c0:r0]


def _rope_partner(w):
    shp = w.shape
    wr = w.reshape(shp[:-1] + (shp[-1] // 2, 2))
    return jnp.stack([-wr[..., 1], wr[..., 0]], axis=-1).reshape(shp)


def _inproj_weights(w_in, w_uq, w_uk, w_uv):
    D = w_in.shape[0]
    o_kr = MLA_Q_LORA + MLA_KV_LORA
    w_kr = w_in[:, o_kr:o_kr + MLA_ROPE]
    z_nope = jnp.zeros((D, MLA_NOPE), F32)
    z_pad = jnp.zeros((D, HEAD_BLOCK - MLA_NOPE - MLA_ROPE), F32)
    krb = jnp.concatenate([z_nope, w_kr, z_pad], axis=1)
    krb_p = jnp.concatenate([z_nope, _rope_partner(w_kr), z_pad], axis=1)
    win = jnp.concatenate([w_in[:, :o_kr], w_in[:, o_kr + MLA_ROPE:], krb, krb_p], axis=1).astype(BF16)

    R = w_uq.shape[0]
    wq = w_uq.reshape(R, MLA_HEADS, MLA_NOPE + MLA_ROPE)
    nope, rope = wq[..., :MLA_NOPE], wq[..., MLA_NOPE:]
    zp = jnp.zeros((R, MLA_HEADS, HEAD_BLOCK - MLA_NOPE - MLA_ROPE), F32)
    zn = jnp.zeros((R, MLA_HEADS, MLA_NOPE), F32)
    q_main = jnp.concatenate([nope, rope, zp], axis=-1).reshape(R, MLA_HEADS * HEAD_BLOCK)
    q_part = jnp.concatenate([zn, _rope_partner(rope), zp], axis=-1).reshape(R, MLA_HEADS * HEAD_BLOCK)
    wq2 = jnp.concatenate([q_main, q_part], axis=1).astype(BF16)

    C = w_uk.shape[0]
    wk = w_uk.reshape(C, MLA_HEADS, MLA_NOPE)
    wk = jnp.concatenate([wk, jnp.zeros((C, MLA_HEADS, HEAD_BLOCK - MLA_NOPE), F32)], axis=-1)
    wv = w_uv.reshape(C, MLA_HEADS // 2, 2, MLA_V)
    zv = jnp.zeros((C, MLA_HEADS // 2, MLA_V), F32)
    wv = jnp.stack([jnp.concatenate([wv[:, :, 0], zv], -1), jnp.concatenate([zv, wv[:, :, 1]], -1)], axis=2)
    wkv = jnp.concatenate([wk.reshape(C, -1), wv.reshape(C, -1)], axis=1).astype(BF16)
    return win, wq2, wkv


def _rope_tables(n, tc):
    rows = n // GRID_W
    pos_r = jnp.repeat(jnp.arange(rows, dtype=F32), GRID_W)
    pos_c = jnp.tile(jnp.arange(GRID_W, dtype=F32), rows)
    half = MLA_ROPE // 2
    inv = jnp.power(ROPE_BASE, -jnp.arange(0, half, 2, dtype=F32) / half)
    ang = jnp.concatenate([pos_r[:, None] * inv, pos_c[:, None] * inv], axis=-1)
    cos2 = jnp.repeat(jnp.cos(ang), 2, axis=-1)
    sin2 = jnp.repeat(jnp.sin(ang), 2, axis=-1)
    pad = HEAD_BLOCK - MLA_NOPE - MLA_ROPE
    cos_l = jnp.concatenate([jnp.ones((n, MLA_NOPE), F32), cos2, jnp.ones((n, pad), F32)], axis=1)
    sin_l = jnp.concatenate([jnp.zeros((n, MLA_NOPE), F32), sin2, jnp.zeros((n, pad), F32)], axis=1)
    cos_t = jnp.concatenate([jnp.ones((tc, HEAD_BLOCK), F32), cos_l], axis=0)
    sin_t = jnp.concatenate([jnp.zeros((tc, HEAD_BLOCK), F32), sin_l], axis=0)
    return cos_t, sin_t


def _inproj(h, mods, win, qn, kvn, wq2, wkv, cos_t, sin_t, B, tpb):
    N, D = h.shape
    ctx_row = mods.shape[0] - 1
    width = MLA_HEADS * HEAD_BLOCK
    const = lambda i: (0, 0)
    return pl.pallas_call(
        _inproj_kernel,
        grid=(N // ROW_TILE,),
        in_specs=[pl.BlockSpec((ROW_TILE, D), lambda i: (i, 0)),
                  pl.BlockSpec((None, 6, D), lambda i: (jnp.where(i % tpb == 0, ctx_row, i // tpb), 0, 0)),
                  pl.BlockSpec(win.shape, const),
                  pl.BlockSpec(qn.shape, const),
                  pl.BlockSpec(kvn.shape, const),
                  pl.BlockSpec(wq2.shape, const),
                  pl.BlockSpec(wkv.shape, const),
                  pl.BlockSpec((ROW_TILE, HEAD_BLOCK), lambda i: (i % tpb, 0)),
                  pl.BlockSpec((ROW_TILE, HEAD_BLOCK), lambda i: (i % tpb, 0))],
        out_specs=[pl.BlockSpec((ROW_TILE, width), lambda i: (i, 0)),
                   pl.BlockSpec((ROW_TILE, width), lambda i: (i, 0)),
                   pl.BlockSpec((ROW_TILE, width), lambda i: (i, 0)),
                   pl.BlockSpec((ROW_TILE, HG_PARTS), lambda i: (i, 0))],
        out_shape=[jax.ShapeDtypeStruct((N, width), BF16),
                   jax.ShapeDtypeStruct((N, width), BF16),
                   jax.ShapeDtypeStruct((N, width), BF16),
                   jax.ShapeDtypeStruct((N, HG_PARTS), F32)],
        compiler_params=_params("arbitrary"),
        name="inproj",
    )(h, mods, win, qn, kvn, wq2, wkv, cos_t, sin_t)


def _attn_kernel(q_ref, k_ref, v_ref, o_ref, *, chunks):
    j = pl.program_id(2)

    def head(hh, n_chunks):
        lo, hi = hh * HEAD_BLOCK, (hh + 1) * HEAD_BLOCK
        q = q_ref[:, lo:hi]
        m = l = acc = None
        for s0, s1 in chunks[:n_chunks]:
            s = lax.dot_general(q, k_ref[s0:s1, lo:hi], (((1,), (1,)), ((), ())),
                                preferred_element_type=F32)
            mc = jnp.max(s, axis=-1, keepdims=True)
            if m is None:
                m = mc
                p = jnp.exp(s - m)
                l = jnp.sum(p, axis=-1, keepdims=True)
                acc = jnp.dot(p.astype(BF16), v_ref[s0:s1, lo:hi], preferred_element_type=F32)
            else:
                m_new = jnp.maximum(m, mc)
                alpha = jnp.exp(m - m_new)
                p = jnp.exp(s - m_new)
                l = alpha * l + jnp.sum(p, axis=-1, keepdims=True)
                acc = alpha * acc + jnp.dot(p.astype(BF16), v_ref[s0:s1, lo:hi],
                                            preferred_element_type=F32)
                m = m_new
        return acc / l

    @pl.when(j == 0)
    def _():
        o_ref[...] = (head(0, 1) + head(1, 1)).astype(o_ref.dtype)

    @pl.when(j > 0)
    def _():
        o_ref[...] = (head(0, len(chunks)) + head(1, len(chunks))).astype(o_ref.dtype)


def _attention(q, k, v, B, tpb, tc):
    N = q.shape[0]
    T = tpb * ROW_TILE
    kc = min(KEY_CHUNK, T - tc)
    chunks = ((0, tc),) + tuple((s, s + kc) for s in range(tc, T, kc))
    pair = 2 * HEAD_BLOCK
    return pl.pallas_call(
        functools.partial(_attn_kernel, chunks=chunks),
        grid=(B, MLA_HEADS // 2, tpb),
        in_specs=[pl.BlockSpec((ROW_TILE, pair), lambda b, p, j: (b * tpb + j, p)),
                  pl.BlockSpec((T, pair), lambda b, p, j: (b, p)),
                  pl.BlockSpec((T, pair), lambda b, p, j: (b, p))],
        out_specs=pl.BlockSpec((ROW_TILE, 2 * MLA_V), lambda b, p, j: (b * tpb + j, p)),
        out_shape=jax.ShapeDtypeStruct((N, MLA_HEADS * MLA_V), BF16),
        compiler_params=_params("arbitrary", "arbitrary", "arbitrary"),
        name="mla_attention",
    )(q, k, v)


def _hgrn_kernel(q_ref, ff_ref, fb_ref, v_ref, g_ref, lb_ref, gn_ref, o_ref, oacc_ref, *, tpb):
    lb = lb_ref[...]
    gn = gn_ref[...]
    C = HG_CHUNK
    per_tile = ROW_TILE // C
    row = lax.broadcasted_iota(jnp.int32, (ROW_TILE, ROW_TILE), 0)
    col = lax.broadcasted_iota(jnp.int32, (ROW_TILE, ROW_TILE), 1)
    same = (row // C) == (col // C)
    ones_blk = same.astype(F32)
    r32 = lax.broadcasted_iota(jnp.int32, (C, C), 0)
    c32 = lax.broadcasted_iota(jnp.int32, (C, C), 1)
    hi = lax.Precision.HIGHEST

    def tile_step(t, st, f_ref, reverse, last_pass):
        r0 = pl.multiple_of(t * ROW_TILE, ROW_TILE)
        rows = pl.ds(r0, ROW_TILE)
        f = lb + (1.0 - lb) * _sigmoid(f_ref[rows, :])
        kk = 1.0 - f
        g = jnp.log(f)
        if reverse:
            cum_m = (same & (col >= row)).astype(F32)
            causal = c32 >= r32
        else:
            cum_m = (same & (col <= row)).astype(F32)
            causal = c32 <= r32
        bcum = jnp.dot(cum_m, g, precision=hi, preferred_element_type=F32)
        btot = jnp.dot(ones_blk, g, precision=hi, preferred_element_type=F32)
        qin = (q_ref[rows, :] * jnp.exp(bcum)).astype(BF16)
        kdec = (kk * jnp.exp(-bcum)).astype(BF16)
        krem = (kk * jnp.exp(btot - bcum)).astype(BF16)
        dec = jnp.exp(btot)
        v = v_ref[rows, :].astype(BF16)
        outs = [None] * per_tile
        order = range(per_tile - 1, -1, -1) if reverse else range(per_tile)
        for c in order:
            sl = slice(c * C, (c + 1) * C)
            a = lax.dot_general(qin[sl], kdec[sl], (((1,), (1,)), ((), ())), preferred_element_type=F32)
            a = jnp.where(causal, a, 0.0).astype(BF16)
            o_c = jnp.dot(a, v[sl], preferred_element_type=F32)
            o_c = o_c + lax.dot_general(qin[sl], st.astype(BF16), (((1,), (1,)), ((), ())),
                                        preferred_element_type=F32)
            u = lax.dot_general(v[sl], krem[sl], (((0,), (0,)), ((), ())), preferred_element_type=F32)
            st = st * dec[c * C:c * C + 1, :] + u
            outs[c] = o_c
        o_t = jnp.concatenate(outs, axis=0)
        if last_pass:
            o_t = o_t + oacc_ref[rows, :]
            gate = g_ref[rows, :]
            y = _rms(o_t, gn) * (gate * _sigmoid(gate))
            o_ref[rows, :] = y.astype(o_ref.dtype)
        else:
            oacc_ref[rows, :] = o_t
        return st

    s0 = jnp.zeros((HG_DK, HG_DK), F32)
    lax.fori_loop(0, tpb, lambda i, st: tile_step(i, st, ff_ref, False, False), s0)
    lax.fori_loop(0, tpb, lambda i, st: tile_step(jnp.where(i == 0, 0, tpb - i), st, fb_ref, True, True), s0)


def _hgrn(hg, lb, gn, B, tpb):
    N = hg.shape[0]
    T = tpb * ROW_TILE
    part = lambda p: pl.BlockSpec((T, HG_DK), lambda b, h: (b, p * HG_HEADS + h))
    return pl.pallas_call(
        functools.partial(_hgrn_kernel, tpb=tpb),
        grid=(B, HG_HEADS),
        in_specs=[part(0), part(1), part(2), part(3), part(4),
                  pl.BlockSpec((None, 1, HG_DK), lambda b, h: (h, 0, 0)),
                  pl.BlockSpec((1, HG_DK), lambda b, h: (0, 0))],
        out_specs=pl.BlockSpec((T, HG_DK), lambda b, h: (b, h)),
        out_shape=jax.ShapeDtypeStruct((N, HG_K), BF16),
        scratch_shapes=[pltpu.VMEM((T, HG_DK), F32)],
        compiler_params=_params("arbitrary", "arbitrary"),
        name="hgrn2",
    )(hg, hg, hg, hg, hg, lb, gn)


def _post_kernel(*refs, n_in):
    xs = refs[:n_in]
    ws = refs[n_in:2 * n_in]
    (bias_ref, h_ref, mod_ref, lg_ref, lb_ref, rw_ref, rb_ref,
     h1_out, bin_out, ti_out, tw_out, cnt_out) = refs[2 * n_in:]
    i = pl.program_id(0)
    y = bias_ref[...]
    for x_ref, w_ref in zip(xs, ws):
        y = y + jnp.dot(x_ref[...], w_ref[...], preferred_element_type=F32)
    m = mod_ref[...]
    h1 = _layernorm(DEEPNORM_ALPHA * h_ref[...] + m[2:3] * y, lg_ref[...], lb_ref[...])
    h1_out[...] = h1
    b_in = h1 * (1.0 + m[4:5]) + m[3:4]
    bin_out[...] = b_in
    lg = jnp.dot(b_in.astype(BF16), rw_ref[...], preferred_element_type=F32) + rb_ref[...]
    lane = lax.broadcasted_iota(jnp.int32, lg.shape, 1)
    vals, idxs = [], []
    member = jnp.zeros(lg.shape, F32)
    for _ in range(TOP_K):
        mx = jnp.max(lg, axis=-1, keepdims=True)
        ix = jnp.min(jnp.where(lg == mx, lane, LANES), axis=-1, keepdims=True)
        hit = lane == ix
        member = member + hit.astype(F32)
        lg = jnp.where(hit, -jnp.inf, lg)
        vals.append(mx)
        idxs.append(ix)
    es = [jnp.exp(v - vals[0]) for v in vals]
    den = es[0] + es[1] + es[2] + es[3]
    ti = jnp.zeros(lg.shape, jnp.int32)
    tw = jnp.zeros(lg.shape, F32)
    for k in range(TOP_K):
        ti = jnp.where(lane == k, idxs[k], ti)
        tw = jnp.where(lane == k, es[k] / den, tw)
    ti_out[...] = ti
    tw_out[...] = tw

    @pl.when(i == 0)
    def _():
        cnt_out[...] = jnp.zeros(cnt_out.shape, F32)

    cnt_out[0:1, :] += jnp.sum(member, axis=0, keepdims=True)


def _post(xs, ws, bias, h3, h_map, mods, mod_map, ln_g, ln_b, rw, rb, n_tok):
    D = D_MODEL
    n_in = len(xs)
    const = lambda i: (0, 0)
    row = lambda i: (i, 0)
    in_specs = ([pl.BlockSpec((ROW_TILE, x.shape[1]), row) for x in xs]
                + [pl.BlockSpec(w.shape, const) for w in ws]
                + [pl.BlockSpec((1, D), const),
                   pl.BlockSpec((None, ROW_TILE, D), h_map),
                   pl.BlockSpec((None, 6, D), mod_map),
                   pl.BlockSpec((1, D), const),
                   pl.BlockSpec((1, D), const),
                   pl.BlockSpec((D, LANES), const),
                   pl.BlockSpec((1, LANES), const)])
    return pl.pallas_call(
        functools.partial(_post_kernel, n_in=n_in),
        grid=(n_tok // ROW_TILE,),
        in_specs=in_specs,
        out_specs=[pl.BlockSpec((ROW_TILE, D), row),
                   pl.BlockSpec((ROW_TILE, D), row),
                   pl.BlockSpec((ROW_TILE, LANES), row),
                   pl.BlockSpec((ROW_TILE, LANES), row),
                   pl.BlockSpec((8, LANES), const)],
        out_shape=[jax.ShapeDtypeStruct((n_tok, D), F32),
                   jax.ShapeDtypeStruct((n_tok, D), F32),
                   jax.ShapeDtypeStruct((n_tok, LANES), jnp.int32),
                   jax.ShapeDtypeStruct((n_tok, LANES), F32),
                   jax.ShapeDtypeStruct((8, LANES), F32)],
        compiler_params=_params("arbitrary"),
        name="post_mixer",
    )(*xs, *ws, bias, h3, mods, ln_g, ln_b, rw, rb)


def _dest_kernel(ti_ref, ps_ref, d_out, carry_ref):
    i = pl.program_id(0)

    @pl.when(i == 0)
    def _():
        carry_ref[...] = jnp.zeros(carry_ref.shape, F32)

    ti = ti_ref[...]
    lane = lax.broadcasted_iota(jnp.int32, ti.shape, 1)
    hits = [lane == ti[:, k:k + 1] for k in range(TOP_K)]
    member = hits[0].astype(F32)
    for k in range(1, TOP_K):
        member = member + hits[k].astype(F32)
    row = lax.broadcasted_iota(jnp.int32, (ROW_TILE, ROW_TILE), 0)
    col = lax.broadcasted_iota(jnp.int32, (ROW_TILE, ROW_TILE), 1)
    before = (col < row).astype(BF16)
    rank = jnp.dot(before, member.astype(BF16), preferred_element_type=F32)
    base = rank + carry_ref[0:1, :] + ps_ref[...]
    dest = jnp.zeros(ti.shape, F32)
    for k in range(TOP_K):
        dk = jnp.sum(jnp.where(hits[k], base, 0.0), axis=-1, keepdims=True)
        dest = jnp.where(lane == k, dk, dest)
    d_out[...] = dest.astype(jnp.int32)
    carry_ref[0:1, :] += jnp.sum(member, axis=0, keepdims=True)


def _dest_rows(ti, pstart):
    n_tok = ti.shape[0]
    return pl.pallas_call(
        _dest_kernel,
        grid=(n_tok // ROW_TILE,),
        in_specs=[pl.BlockSpec((ROW_TILE, LANES), lambda i: (i, 0)),
                  pl.BlockSpec((1, LANES), lambda i: (0, 0))],
        out_specs=pl.BlockSpec((ROW_TILE, LANES), lambda i: (i, 0)),
        out_shape=jax.ShapeDtypeStruct((n_tok, LANES), jnp.int32),
        scratch_shapes=[pltpu.VMEM((8, LANES), F32)],
        compiler_params=_params("arbitrary"),
        name="moe_dest",
    )(ti, pstart)


def _row_copy(src, dst, sem):
    return pltpu.make_async_copy(src, dst, sem)


def _dispatch_kernel(bv_ref, dest_ref, x_ref, xs_hbm, zero_ref, sem, zsem):
    n_blocks = xs_hbm.shape[0] // MOE_TILE

    @pl.when(pl.program_id(0) == 0)
    def _():
        zero_ref[...] = jnp.zeros(zero_ref.shape, F32)

        def fill(wait):
            def body(j, carry):
                @pl.when(bv_ref[j] < MOE_TILE)
                def _():
                    rows = pl.ds(pl.multiple_of(j * MOE_TILE, MOE_TILE), MOE_TILE)
                    cp = _row_copy(zero_ref, xs_hbm.at[rows, :], zsem)
                    if wait:
                        cp.wait()
                    else:
                        cp.start()
                return carry
            return body

        lax.fori_loop(0, n_blocks, fill(False), 0)
        lax.fori_loop(0, n_blocks, fill(True), 0)

    def start(r, carry):
        for k in range(TOP_K):
            d = dest_ref[r * TOP_K + k]
            _row_copy(x_ref.at[pl.ds(r, 1), :], xs_hbm.at[pl.ds(d, 1), :], sem).start()
        return carry

    lax.fori_loop(0, ROW_TILE, start, 0)

    def wait(r, carry):
        for k in range(TOP_K):
            _row_copy(x_ref.at[pl.ds(0, 1), :], xs_hbm.at[pl.ds(0, 1), :], sem).wait()
        return carry

    lax.fori_loop(0, ROW_TILE, wait, 0)


def _dispatch(block_valid, dest_flat, b_in, n_rows):
    n_tok, D = b_in.shape
    grid_spec = pltpu.PrefetchScalarGridSpec(
        num_scalar_prefetch=1,
        grid=(n_tok // ROW_TILE,),
        in_specs=[pl.BlockSpec((ROW_TILE * TOP_K,), lambda i, bv: (i,), memory_space=pltpu.SMEM),
                  pl.BlockSpec((ROW_TILE, D), lambda i, bv: (i, 0))],
        out_specs=pl.BlockSpec(memory_space=pl.ANY),
        scratch_shapes=[pltpu.VMEM((MOE_TILE, D), F32), pltpu.SemaphoreType.DMA(()),
                        pltpu.SemaphoreType.DMA(())],
    )
    return pl.pallas_call(
        _dispatch_kernel,
        grid_spec=grid_spec,
        out_shape=jax.ShapeDtypeStruct((n_rows, D), F32),
        compiler_params=_params("arbitrary"),
        name="moe_dispatch",
    )(block_valid, dest_flat, b_in)


def _expert_kernel(be_ref, bv_ref, x_ref, wg_ref, bg_ref, wu_ref, bu_ref, wd_ref, bd_ref, y_ref):
    j = pl.program_id(0)
    valid = bv_ref[j]

    @pl.when(valid > 0)
    def _():
        rowi = lax.broadcasted_iota(jnp.int32, x_ref.shape, 0)
        x = jnp.where(rowi < valid, x_ref[...], 0.0).astype(BF16)
        g = jnp.dot(x, wg_ref[...], preferred_element_type=F32) + bg_ref[...]
        u = jnp.dot(x, wu_ref[...], preferred_element_type=F32) + bu_ref[...]
        g = jnp.minimum(g, SWIGLU_LIMIT)
        u = jnp.clip(u, -SWIGLU_LIMIT, SWIGLU_LIMIT)
        act = ((u + 1.0) * g * _sigmoid(SWIGLU_ALPHA * g)).astype(BF16)
        y_ref[...] = jnp.dot(act, wd_ref[...], preferred_element_type=F32) + bd_ref[...]

    @pl.when(valid <= 0)
    def _():
        y_ref[...] = jnp.zeros(y_ref.shape, F32)


def _experts(block_e, block_valid, xs, wg, bg, wu, bu, wd, bd):
    P, D = xs.shape
    E, _, DE = wg.shape
    grid_spec = pltpu.PrefetchScalarGridSpec(
        num_scalar_prefetch=2,
        grid=(P // MOE_TILE,),
        in_specs=[pl.BlockSpec((MOE_TILE, D), lambda j, be, bv: (j, 0)),
                  pl.BlockSpec((None, D, DE), lambda j, be, bv: (be[j], 0, 0)),
                  pl.BlockSpec((None, 1, DE), lambda j, be, bv: (be[j], 0, 0)),
                  pl.BlockSpec((None, D, DE), lambda j, be, bv: (be[j], 0, 0)),
                  pl.BlockSpec((None, 1, DE), lambda j, be, bv: (be[j], 0, 0)),
                  pl.BlockSpec((None, DE, D), lambda j, be, bv: (be[j], 0, 0)),
                  pl.BlockSpec((None, 1, D), lambda j, be, bv: (be[j], 0, 0))],
        out_specs=pl.BlockSpec((MOE_TILE, D), lambda j, be, bv: (j, 0)),
    )
    return pl.pallas_call(
        _expert_kernel,
        grid_spec=grid_spec,
        out_shape=jax.ShapeDtypeStruct((P, D), F32),
        compiler_params=_params("arbitrary"),
        name="moe_experts",
    )(block_e, block_valid, xs, wg, bg, wu, bu, wd, bd)


def _combine_kernel(dest_ref, ys_hbm, tw_ref, h_ref, mod_ref, lg_ref, lb_ref, o_ref, buf_ref, sem):
    def start(r, carry):
        for k in range(TOP_K):
            d = dest_ref[r * TOP_K + k]
            _row_copy(ys_hbm.at[pl.ds(d, 1), :], buf_ref.at[k, pl.ds(r, 1), :], sem).start()
        return carry

    lax.fori_loop(0, ROW_TILE, start, 0)

    def wait(r, carry):
        for k in range(TOP_K):
            _row_copy(ys_hbm.at[pl.ds(0, 1), :], buf_ref.at[0, pl.ds(0, 1), :], sem).wait()
        return carry

    lax.fori_loop(0, ROW_TILE, wait, 0)

    tw = tw_ref[...]
    f = tw[:, 0:1] * buf_ref[0]
    for k in range(1, TOP_K):
        f = f + tw[:, k:k + 1] * buf_ref[k]
    m = mod_ref[...]
    o_ref[...] = _layernorm(DEEPNORM_ALPHA * h_ref[...] + m[5:6] * f, lg_ref[...], lb_ref[...])


def _combine(dest_flat, ys, tw, h1, mods, mod_map, ln_g, ln_b):
    n_tok, D = h1.shape
    const = lambda i: (0, 0)
    return pl.pallas_call(
        _combine_kernel,
        grid=(n_tok // ROW_TILE,),
        in_specs=[pl.BlockSpec((ROW_TILE * TOP_K,), lambda i: (i,), memory_space=pltpu.SMEM),
                  pl.BlockSpec(memory_space=pl.ANY),
                  pl.BlockSpec((ROW_TILE, LANES), lambda i: (i, 0)),
                  pl.BlockSpec((ROW_TILE, D), lambda i: (i, 0)),
                  pl.BlockSpec((None, 6, D), mod_map),
                  pl.BlockSpec((1, D), const),
                  pl.BlockSpec((1, D), const)],
        out_specs=pl.BlockSpec((ROW_TILE, D), lambda i: (i, 0)),
        out_shape=jax.ShapeDtypeStruct((n_tok, D), F32),
        scratch_shapes=[pltpu.VMEM((TOP_K, ROW_TILE, D), F32), pltpu.SemaphoreType.DMA(())],
        compiler_params=_params("arbitrary"),
        name="moe_combine",
    )(dest_flat, ys, tw, h1, mods, ln_g, ln_b)


def _moe(b_in, ti, tw, counts, h1, mods, mod_map, ln_g, ln_b, wg, bg, wu, bu, wd, bd):
    n_tok = b_in.shape[0]
    cnt = counts[0, :N_EXPERTS].astype(jnp.int32)
    pcnt = (cnt + MOE_TILE - 1) // MOE_TILE * MOE_TILE
    pends = jnp.cumsum(pcnt)
    pstart = pends - pcnt
    n_blocks = -(-(n_tok * TOP_K + N_EXPERTS * (MOE_TILE - 1)) // MOE_TILE)
    blk0 = jnp.arange(n_blocks, dtype=jnp.int32) * MOE_TILE
    block_e = jnp.minimum(jnp.searchsorted(pends, blk0, side='right'), N_EXPERTS - 1).astype(jnp.int32)
    block_valid = jnp.clip(cnt[block_e] - (blk0 - pstart[block_e]), 0, MOE_TILE).astype(jnp.int32)
    ps_row = jnp.zeros((1, LANES), F32).at[0, :N_EXPERTS].set(pstart.astype(F32))
    dest = _dest_rows(ti, ps_row)
    dest_flat = dest[:, :TOP_K].reshape(-1)
    xs = _dispatch(block_valid, dest_flat, b_in, n_blocks * MOE_TILE)
    ys = _experts(block_e, block_valid, xs, wg, bg, wu, bu, wd, bd)
    return _combine(dest_flat, ys, tw, h1, mods, mod_map, ln_g, ln_b)


def _expert_weights(w_gu, b_gu, w_down, b_down):
    E = w_gu.shape[0]
    wg = w_gu[:, :, 0::2].astype(BF16)
    wu = w_gu[:, :, 1::2].astype(BF16)
    bg = b_gu[:, 0::2].reshape(E, 1, -1)
    bu = b_gu[:, 1::2].reshape(E, 1, -1)
    return wg, bg, wu, bu, w_down.astype(BF16), b_down.reshape(E, 1, -1)


def _router_weights(router_w, router_b):
    D = router_w.shape[0]
    rw = jnp.zeros((D, LANES), F32).at[:, :N_EXPERTS].set(router_w).astype(BF16)
    rb = jnp.full((1, LANES), -jnp.inf, F32).at[0, :N_EXPERTS].set(router_b)
    return rw, rb


def _glu_kernel(h_ref, mod_ref, w_ref, b_ref, u_out):
    m = mod_ref[...]
    a = (h_ref[...] * (1.0 + m[1:2]) + m[0:1]).astype(BF16)
    z = jnp.dot(a, w_ref[...], preferred_element_type=F32) + b_ref[...]
    half = z.shape[1] // 2
    u_out[...] = z[:, :half] * _sigmoid(z[:, half:])


def _glu(h3, mods, w, b, B, tpb_lat):
    D = D_MODEL
    return pl.pallas_call(
        _glu_kernel,
        grid=(B, tpb_lat),
        in_specs=[pl.BlockSpec((None, ROW_TILE, D), lambda b, j: (b, j + 1, 0)),
                  pl.BlockSpec((None, 6, D), lambda b, j: (b, 0, 0)),
                  pl.BlockSpec(w.shape, lambda b, j: (0, 0)),
                  pl.BlockSpec((1, w.shape[1]), lambda b, j: (0, 0))],
        out_specs=pl.BlockSpec((None, ROW_TILE, D), lambda b, j: (b, j, 0)),
        out_shape=jax.ShapeDtypeStruct((B, tpb_lat * ROW_TILE, D), F32),
        compiler_params=_params("arbitrary", "arbitrary"),
        name="conv_glu",
    )(h3, mods, w, b)


def _dwconv_kernel(prev_ref, cur_ref, next_ref, w_ref, b_ref, lg_ref, lb_ref, o_ref, win_ref, *, tpb_lat):
    j = pl.program_id(1)
    H = CONV_HALO
    zero = jnp.zeros((H, cur_ref.shape[1]), F32)
    win_ref[0:H, :] = jnp.where(j > 0, prev_ref[...], zero)
    win_ref[H:H + ROW_TILE, :] = cur_ref[...]
    win_ref[H + ROW_TILE:, :] = jnp.where(j < tpb_lat - 1, next_ref[...], zero)
    w = w_ref[...]
    off = H - CONV_WIDTH // 2
    acc = jnp.zeros(cur_ref.shape, F32) + b_ref[...]
    for k in range(CONV_WIDTH):
        acc = acc + win_ref[off + k:off + k + ROW_TILE, :] * w[k:k + 1, :]
    y = _layernorm(acc, lg_ref[...], lb_ref[...])
    o_ref[...] = (y * _sigmoid(y)).astype(o_ref.dtype)


def _dwconv(u, w, b, ln_g, ln_b, B, tpb_lat):
    D = D_MODEL
    n = tpb_lat * ROW_TILE
    per = ROW_TILE // CONV_HALO
    last = n // CONV_HALO - 1
    const = lambda b_, j: (0, 0)
    return pl.pallas_call(
        functools.partial(_dwconv_kernel, tpb_lat=tpb_lat),
        grid=(B, tpb_lat),
        in_specs=[pl.BlockSpec((None, CONV_HALO, D), lambda b_, j: (b_, jnp.maximum(j * per - 1, 0), 0)),
                  pl.BlockSpec((None, ROW_TILE, D), lambda b_, j: (b_, j, 0)),
                  pl.BlockSpec((None, CONV_HALO, D), lambda b_, j: (b_, jnp.minimum((j + 1) * per, last), 0)),
                  pl.BlockSpec(w.shape, const),
                  pl.BlockSpec((1, D), const),
                  pl.BlockSpec((1, D), const),
                  pl.BlockSpec((1, D), const)],
        out_specs=pl.BlockSpec((ROW_TILE, D), lambda b_, j: (b_ * tpb_lat + j, 0)),
        out_shape=jax.ShapeDtypeStruct((B * n, D), BF16),
        scratch_shapes=[pltpu.VMEM((ROW_TILE + 2 * CONV_HALO, D), F32)],
        compiler_params=_params("arbitrary", "arbitrary"),
        name="conv_dw",
    )(u, u, u, w, b, ln_g, ln_b)


def kernel(x, c, ctx, c_ctx, mod_w, mod_b, ln1_g, ln1_b, ln2_g, ln2_b, mix_w_in, mla_q_norm, mla_kv_norm, mla_w_uq, mla_w_uk, mla_w_uv, hg_lb_logits, hg_norm, mix_w_out, conv_pw1_w, conv_pw1_b, conv_dw_w, conv_dw_b, conv_ln_g, conv_ln_b, conv_pw2_w, conv_pw2_b, router_w, router_b, exp_w_gu, exp_b_gu, exp_w_down, exp_b_down):
    B, n, D = x.shape
    tc = ctx.shape[1]
    assert D == D_MODEL and tc == ROW_TILE and n % ROW_TILE == 0 and n % GRID_W == 0
    assert B + 1 <= 16
    T = tc + n
    tpb = T // ROW_TILE
    tpb_lat = n // ROW_TILE
    N = B * T
    row2 = lambda v: v.reshape(1, -1)

    cond = jnp.zeros((16, D), F32).at[:B].set(c).at[B].set(c_ctx)
    mods = _modulation(cond, mod_w, mod_b)[:, :B + 1].reshape(DEPTH, B + 1, 6, D)
    lb_all = jnp.cumsum(jax.nn.softmax(hg_lb_logits.astype(F32), axis=0), axis=0)

    h0 = jnp.concatenate([ctx, x], axis=1)
    win, wq2, wkv = _inproj_weights(mix_w_in[0], mla_w_uq[0], mla_w_uk[0], mla_w_uv[0])
    cos_t, sin_t = _rope_tables(n, tc)
    q, k, v, hg = _inproj(h0.reshape(N, D), mods[0], win, row2(mla_q_norm[0]), row2(mla_kv_norm[0]),
                          wq2, wkv, cos_t, sin_t, B, tpb)
    att = _attention(q, k, v, B, tpb, tc)
    hgo = _hgrn(hg, lb_all[0].reshape(HG_HEADS, 1, HG_DK), row2(hg_norm[0]), B, tpb)
    w_out = mix_w_out[0].astype(BF16)
    n_att = MLA_HEADS * MLA_V
    rw, rb = _router_weights(router_w[0], router_b[0])
    mod_map0 = lambda i: (jnp.where(i % tpb == 0, B, i // tpb), 0, 0)
    h1, b_in, ti, tw, counts = _post(
        [att, hgo], [w_out[:n_att], w_out[n_att:]], jnp.zeros((1, D), F32),
        h0.reshape(N // ROW_TILE, ROW_TILE, D), lambda i: (i, 0, 0), mods[0], mod_map0,
        row2(ln1_g[0]), row2(ln1_b[0]), rw, rb, N)
    ew = _expert_weights(exp_w_gu[0], exp_b_gu[0], exp_w_down[0], exp_b_down[0])
    h2 = _moe(b_in, ti, tw, counts, h1, mods[0], mod_map0, row2(ln2_g[0]), row2(ln2_b[0]), *ew)

    n_lat = B * n
    h2_3 = h2.reshape(B * tpb, ROW_TILE, D)
    u = _glu(h2.reshape(B, T, D), mods[1], conv_pw1_w[0].astype(BF16), row2(conv_pw1_b[0]), B, tpb_lat)
    cv = _dwconv(u, conv_dw_w[0], row2(conv_dw_b[0]), row2(conv_ln_g[0]), row2(conv_ln_b[0]), B, tpb_lat)
    rw, rb = _router_weights(router_w[1], router_b[1])
    mod_map1 = lambda i: (i // tpb_lat, 0, 0)
    h_map1 = lambda i: ((i // tpb_lat) * tpb + i % tpb_lat + 1, 0, 0)
    h3, b_in, ti, tw, counts = _post(
        [cv], [conv_pw2_w[0].astype(BF16)], row2(conv_pw2_b[0]),
        h2_3, h_map1, mods[1], mod_map1, row2(ln1_g[1]), row2(ln1_b[1]), rw, rb, n_lat)
    ew = _expert_weights(exp_w_gu[1], exp_b_gu[1], exp_w_down[1], exp_b_down[1])
    h4 = _moe(b_in, ti, tw, counts, h3, mods[1], mod_map1, row2(ln2_g[1]), row2(ln2_b[1]), *ew)
    return h4.reshape(B, n, D)
```

```python
import functools

import numpy as np
import jax
import jax.numpy as jnp
from jax import lax
from jax.experimental import pallas as pl
from jax.experimental.pallas import tpu as pltpu

F32 = jnp.float32
BF16 = jnp.bfloat16

D_MODEL = 1024
DEPTH = 2
GRID_W = 64
MLA_HEADS = 8
MLA_Q_LORA = 384
MLA_KV_LORA = 256
MLA_NOPE = 64
MLA_ROPE = 32
MLA_V = 64
MLA_SCALE = (MLA_NOPE + MLA_ROPE) ** -0.5
LOG2_E = 1.4426950408889634
ROPE_BASE = 10000.0
HG_HEADS = 4
HG_DK = 128
HG_CHUNK = 32
HG_K = HG_HEADS * HG_DK
CONV_WIDTH = 31
N_EXPERTS = 32
TOP_K = 4
SWIGLU_LIMIT = 7.0
SWIGLU_ALPHA = 1.702
LN_EPS = 1e-5
RMS_EPS = 1e-6
DEEPNORM_ALPHA = (2.0 * DEPTH) ** 0.25

LANES = 128
ROW_TILE = 256
HEAD_BLOCK = 128
KEY_CHUNK = 1024
MOE_TILE = 256
CONV_HALO = 16
IN_WIDTH = 3456
HG_PARTS = 5 * HG_K
VMEM_LIMIT = 56 * 1024 * 1024


def _params(*sem):
    return pltpu.CompilerParams(dimension_semantics=sem, vmem_limit_bytes=VMEM_LIMIT)


def _sigmoid(x):
    return 1.0 / (1.0 + jnp.exp(-x))


def _layernorm(x, g, b):
    xc = x - jnp.mean(x, axis=-1, keepdims=True)
    var = jnp.mean(xc * xc, axis=-1, keepdims=True)
    return xc * lax.rsqrt(var + LN_EPS) * g + b


def _rms(x, g):
    return x * lax.rsqrt(jnp.mean(x * x, axis=-1, keepdims=True) + RMS_EPS) * g


def _mod_kernel(c_ref, w_ref, b_ref, o_ref):
    c = c_ref[...]
    s = (c * _sigmoid(c)).astype(BF16)
    o_ref[...] = jnp.dot(s, w_ref[...].astype(BF16), preferred_element_type=F32) + b_ref[...]


def _modulation(cond, mod_w, mod_b):
    L, D, W = mod_w.shape
    R = cond.shape[0]
    return pl.pallas_call(
        _mod_kernel,
        grid=(L, W // D),
        in_specs=[pl.BlockSpec((R, D), lambda l, n: (0, 0)),
                  pl.BlockSpec((None, D, D), lambda l, n: (l, 0, n)),
                  pl.BlockSpec((None, 1, D), lambda l, n: (l, 0, n))],
        out_specs=pl.BlockSpec((None, R, D), lambda l, n: (l, 0, n)),
        out_shape=jax.ShapeDtypeStruct((L, R, W), F32),
        compiler_params=_params("arbitrary", "arbitrary"),
        name="adaln_mod",
    )(cond, mod_w, mod_b.reshape(L, 1, W))


def _inproj_kernel(h_ref, mod_ref, win_ref, qn_ref, kvn_ref, wq_ref, wkv_ref, c_ref, s_ref,
                   q_out, k_out, v_out, hg_out):
    m = mod_ref[...]
    a = (h_ref[...] * (1.0 + m[1:2]) + m[0:1]).astype(BF16)
    z = jnp.dot(a, win_ref[...], preferred_element_type=F32)
    cos = c_ref[...]
    sin = s_ref[...]
    cos_h = jnp.concatenate([cos] * MLA_HEADS, axis=1)
    sin_h = jnp.concatenate([sin] * MLA_HEADS, axis=1)
    width = MLA_HEADS * HEAD_BLOCK

    cqn = _rms(z[:, :MLA_Q_LORA], qn_ref[...]).astype(BF16)
    qq = jnp.dot(cqn, wq_ref[...], preferred_element_type=F32)
    q_out[...] = ((qq[:, :width] * cos_h + qq[:, width:] * sin_h) * (MLA_SCALE * LOG2_E)).astype(BF16)

    c0 = MLA_Q_LORA + MLA_KV_LORA
    ckvn = _rms(z[:, MLA_Q_LORA:c0], kvn_ref[...]).astype(BF16)
    kv = jnp.dot(ckvn, wkv_ref[...], preferred_element_type=F32)
    r0 = c0 + HG_PARTS
    kr = z[:, r0:r0 + HEAD_BLOCK] * cos + z[:, r0 + HEAD_BLOCK:r0 + 2 * HEAD_BLOCK] * sin
    k_out[...] = (kv[:, :width] + jnp.concatenate([kr] * MLA_HEADS, axis=1)).astype(BF16)
    v_out[...] = kv[:, width:].astype(BF16)
    hg_out[...] = z[:, c0:r0]


def _rope_partner(w):
    shp = w.shape
    wr = w.reshape(shp[:-1] + (shp[-1] // 2, 2))
    return jnp.stack([-wr[..., 1], wr[..., 0]], axis=-1).reshape(shp)


def _inproj_weights(w_in, w_uq, w_uk, w_uv):
    D = w_in.shape[0]
    o_kr = MLA_Q_LORA + MLA_KV_LORA
    w_kr = w_in[:, o_kr:o_kr + MLA_ROPE]
    z_nope = jnp.zeros((D, MLA_NOPE), F32)
    z_pad = jnp.zeros((D, HEAD_BLOCK - MLA_NOPE - MLA_ROPE), F32)
    krb = jnp.concatenate([z_nope, w_kr, z_pad], axis=1)
    krb_p = jnp.concatenate([z_nope, _rope_partner(w_kr), z_pad], axis=1)
    win = jnp.concatenate([w_in[:, :o_kr], w_in[:, o_kr + MLA_ROPE:], krb, krb_p], axis=1).astype(BF16)

    R = w_uq.shape[0]
    wq = w_uq.reshape(R, MLA_HEADS, MLA_NOPE + MLA_ROPE)
    nope, rope = wq[..., :MLA_NOPE], wq[..., MLA_NOPE:]
    zp = jnp.zeros((R, MLA_HEADS, HEAD_BLOCK - MLA_NOPE - MLA_ROPE), F32)
    zn = jnp.zeros((R, MLA_HEADS, MLA_NOPE), F32)
    q_main = jnp.concatenate([nope, rope, zp], axis=-1).reshape(R, MLA_HEADS * HEAD_BLOCK)
    q_part = jnp.concatenate([zn, _rope_partner(rope), zp], axis=-1).reshape(R, MLA_HEADS * HEAD_BLOCK)
    wq2 = jnp.concatenate([q_main, q_part], axis=1).astype(BF16)

    C = w_uk.shape[0]
    wk = w_uk.reshape(C, MLA_HEADS, MLA_NOPE)
    wk = jnp.concatenate([wk, jnp.zeros((C, MLA_HEADS, HEAD_BLOCK - MLA_NOPE), F32)], axis=-1)
    wv = w_uv.reshape(C, MLA_HEADS // 2, 2, MLA_V)
    zv = jnp.zeros((C, MLA_HEADS // 2, MLA_V), F32)
    wv = jnp.stack([jnp.concatenate([wv[:, :, 0], zv], -1), jnp.concatenate([zv, wv[:, :, 1]], -1)], axis=2)
    wkv = jnp.concatenate([wk.reshape(C, -1), wv.reshape(C, -1)], axis=1).astype(BF16)
    return win, wq2, wkv


def _rope_tables(n, tc):
    rows = n // GRID_W
    pos_r = jnp.repeat(jnp.arange(rows, dtype=F32), GRID_W)
    pos_c = jnp.tile(jnp.arange(GRID_W, dtype=F32), rows)
    half = MLA_ROPE // 2
    inv = jnp.power(ROPE_BASE, -jnp.arange(0, half, 2, dtype=F32) / half)
    ang = jnp.concatenate([pos_r[:, None] * inv, pos_c[:, None] * inv], axis=-1)
    cos2 = jnp.repeat(jnp.cos(ang), 2, axis=-1)
    sin2 = jnp.repeat(jnp.sin(ang), 2, axis=-1)
    pad = HEAD_BLOCK - MLA_NOPE - MLA_ROPE
    cos_l = jnp.concatenate([jnp.ones((n, MLA_NOPE), F32), cos2, jnp.ones((n, pad), F32)], axis=1)
    sin_l = jnp.concatenate([jnp.zeros((n, MLA_NOPE), F32), sin2, jnp.zeros((n, pad), F32)], axis=1)
    cos_t = jnp.concatenate([jnp.ones((tc, HEAD_BLOCK), F32), cos_l], axis=0)
    sin_t = jnp.concatenate([jnp.zeros((tc, HEAD_BLOCK), F32), sin_l], axis=0)
    return cos_t, sin_t


def _inproj(h, mods, win, qn, kvn, wq2, wkv, cos_t, sin_t, B, tpb):
    N, D = h.shape
    ctx_row = mods.shape[0] - 1
    width = MLA_HEADS * HEAD_BLOCK
    const = lambda i: (0, 0)
    return pl.pallas_call(
        _inproj_kernel,
        grid=(N // ROW_TILE,),
        in_specs=[pl.BlockSpec((ROW_TILE, D), lambda i: (i, 0)),
                  pl.BlockSpec((None, 6, D), lambda i: (jnp.where(i % tpb == 0, ctx_row, i // tpb), 0, 0)),
                  pl.BlockSpec(win.shape, const),
                  pl.BlockSpec(qn.shape, const),
                  pl.BlockSpec(kvn.shape, const),
                  pl.BlockSpec(wq2.shape, const),
                  pl.BlockSpec(wkv.shape, const),
                  pl.BlockSpec((ROW_TILE, HEAD_BLOCK), lambda i: (i % tpb, 0)),
                  pl.BlockSpec((ROW_TILE, HEAD_BLOCK), lambda i: (i % tpb, 0))],
        out_specs=[pl.BlockSpec((ROW_TILE, width), lambda i: (i, 0)),
                   pl.BlockSpec((ROW_TILE, width), lambda i: (i, 0)),
                   pl.BlockSpec((ROW_TILE, width), lambda i: (i, 0)),
                   pl.BlockSpec((ROW_TILE, HG_PARTS), lambda i: (i, 0))],
        out_shape=[jax.ShapeDtypeStruct((N, width), BF16),
                   jax.ShapeDtypeStruct((N, width), BF16),
                   jax.ShapeDtypeStruct((N, width), BF16),
                   jax.ShapeDtypeStruct((N, HG_PARTS), F32)],
        compiler_params=_params("arbitrary"),
        name="inproj",
    )(h, mods, win, qn, kvn, wq2, wkv, cos_t, sin_t)


ATTN_HEADS = 4


def _attn_kernel(q_ref, k_ref, v_ref, o_ref, *, chunks):
    j = pl.program_id(2)

    def head(hh, n_chunks):
        lo, hi = hh * HEAD_BLOCK, (hh + 1) * HEAD_BLOCK
        q = q_ref[:, lo:hi]
        ss = [lax.dot_general(q, k_ref[s0:s1, lo:hi], (((1,), (1,)), ((), ())), preferred_element_type=F32)
              for s0, s1 in chunks[:n_chunks]]
        m = jnp.max(ss[0], axis=-1, keepdims=True)
        for s in ss[1:]:
            m = jnp.maximum(m, jnp.max(s, axis=-1, keepdims=True))
        l = acc = None
        for s, (s0, s1) in zip(ss, chunks[:n_chunks]):
            p = jnp.exp2(s - m)
            lc = jnp.sum(p, axis=-1, keepdims=True)
            ac = jnp.dot(p.astype(BF16), v_ref[s0:s1, lo:hi], preferred_element_type=F32)
            l = lc if l is None else l + lc
            acc = ac if acc is None else acc + ac
        return acc / l

    def pairs(n_chunks):
        outs = [head(hh, n_chunks) + head(hh + 1, n_chunks) for hh in range(0, ATTN_HEADS, 2)]
        o_ref[...] = jnp.concatenate(outs, axis=1).astype(o_ref.dtype)

    @pl.when(j == 0)
    def _():
        pairs(1)

    @pl.when(j > 0)
    def _():
        pairs(len(chunks))


def _attention(q, k, v, B, tpb, tc):
    N = q.shape[0]
    T = tpb * ROW_TILE
    kc = min(KEY_CHUNK, T - tc)
    chunks = ((0, tc),) + tuple((s, s + kc) for s in range(tc, T, kc))
    width = ATTN_HEADS * HEAD_BLOCK
    return pl.pallas_call(
        functools.partial(_attn_kernel, chunks=chunks),
        grid=(B, MLA_HEADS // ATTN_HEADS, tpb),
        in_specs=[pl.BlockSpec((ROW_TILE, width), lambda b, p, j: (b * tpb + j, p)),
                  pl.BlockSpec((T, width), lambda b, p, j: (b, p)),
                  pl.BlockSpec((T, width), lambda b, p, j: (b, p))],
        out_specs=pl.BlockSpec((ROW_TILE, ATTN_HEADS * MLA_V), lambda b, p, j: (b * tpb + j, p)),
        out_shape=jax.ShapeDtypeStruct((N, MLA_HEADS * MLA_V), BF16),
        compiler_params=_params("arbitrary", "arbitrary", "arbitrary"),
        name="mla_attention",
    )(q, k, v)


def _hgrn_kernel(q_ref, ff_ref, fb_ref, v_ref, g_ref, lb_ref, gn_ref, o_ref, oacc_ref, obwd_ref, *, tpb):
    lb = lb_ref[...]
    gn = gn_ref[...]
    C = HG_CHUNK
    per_tile = ROW_TILE // C
    row = lax.broadcasted_iota(jnp.int32, (ROW_TILE, ROW_TILE), 0)
    col = lax.broadcasted_iota(jnp.int32, (ROW_TILE, ROW_TILE), 1)
    same = (row // C) == (col // C)
    ones_blk = same.astype(BF16)
    row_chunk = lax.broadcasted_iota(jnp.int32, (ROW_TILE, HG_DK), 0) // C
    nt_dims = (((1,), (1,)), ((), ()))
    tn_dims = (((0,), (0,)), ((), ()))

    def tile_rows(t):
        return pl.ds(pl.multiple_of(t * ROW_TILE, ROW_TILE), ROW_TILE)

    def chunk_sums(m01, g3):
        return sum(jnp.dot(m01, part, preferred_element_type=F32) for part in g3)

    def per_chunk_blocks(x):
        zero = jnp.zeros(x.shape, x.dtype)
        return jnp.concatenate([jnp.where(row_chunk == c, x, zero) for c in range(per_tile)], axis=1)

    def tile_step(t, st, f_ref, reverse, acc_ref):
        rows = tile_rows(t)
        f = lb + (1.0 - lb) * _sigmoid(f_ref[rows, :])
        kk = 1.0 - f
        g = jnp.log(f)
        g_hi = g.astype(BF16)
        r1 = g - g_hi.astype(F32)
        g_mid = r1.astype(BF16)
        g3 = (g_hi, g_mid, (r1 - g_mid.astype(F32)).astype(BF16))
        in_chunk = same & ((col >= row) if reverse else (col <= row))
        bcum = chunk_sums(in_chunk.astype(BF16), g3)
        btot = chunk_sums(ones_blk, g3)
        qin = (q_ref[rows, :] * jnp.exp(bcum)).astype(BF16)
        kdec = (kk * jnp.exp(-bcum)).astype(BF16)
        krem = (kk * jnp.exp(btot - bcum)).astype(BF16)
        dec = jnp.exp(btot)
        v = v_ref[rows, :].astype(BF16)
        a = lax.dot_general(qin, kdec, nt_dims, preferred_element_type=F32)
        a = jnp.where(in_chunk, a, 0.0).astype(BF16)
        o_t = jnp.dot(a, v, preferred_element_type=F32)
        ut = lax.dot_general(v, per_chunk_blocks(krem), tn_dims, preferred_element_type=F32)
        states = [None] * per_tile
        for c in (range(per_tile - 1, -1, -1) if reverse else range(per_tile)):
            states[c] = st.astype(BF16)
            st = st * dec[c * C:c * C + 1, :] + ut[:, c * HG_DK:(c + 1) * HG_DK]
        o_t = o_t + lax.dot_general(per_chunk_blocks(qin), jnp.concatenate(states, axis=1), nt_dims,
                                    preferred_element_type=F32)
        acc_ref[rows, :] = o_t
        return st

    def both(i, sts):
        st_f = tile_step(i, sts[0], ff_ref, False, oacc_ref)
        st_b = tile_step(jnp.where(i == 0, 0, tpb - i), sts[1], fb_ref, True, obwd_ref)
        return st_f, st_b

    s0 = jnp.zeros((HG_DK, HG_DK), F32)
    lax.fori_loop(0, tpb, both, (s0, s0))

    def readout(t, carry):
        rows = tile_rows(t)
        gate = g_ref[rows, :]
        y = _rms(oacc_ref[rows, :] + obwd_ref[rows, :], gn) * (gate * _sigmoid(gate))
        o_ref[rows, :] = y.astype(o_ref.dtype)
        return carry

    lax.fori_loop(0, tpb, readout, 0)


def _hgrn(hg, lb, gn, B, tpb):
    N = hg.shape[0]
    T = tpb * ROW_TILE
    part = lambda p: pl.BlockSpec((T, HG_DK), lambda b, h: (b, p * HG_HEADS + h))
    return pl.pallas_call(
        functools.partial(_hgrn_kernel, tpb=tpb),
        grid=(B, HG_HEADS),
        in_specs=[part(0), part(1), part(2), part(3), part(4),
                  pl.BlockSpec((None, 1, HG_DK), lambda b, h: (h, 0, 0)),
                  pl.BlockSpec((1, HG_DK), lambda b, h: (0, 0))],
        out_specs=pl.BlockSpec((T, HG_DK), lambda b, h: (b, h)),
        out_shape=jax.ShapeDtypeStruct((N, HG_K), BF16),
        scratch_shapes=[pltpu.VMEM((T, HG_DK), F32), pltpu.VMEM((T, HG_DK), F32)],
        compiler_params=_params("arbitrary", "arbitrary"),
        name="hgrn2",
    )(hg, hg, hg, hg, hg, lb, gn)


def _post_kernel(*refs, n_in):
    xs = refs[:n_in]
    ws = refs[n_in:2 * n_in]
    (bias_ref, h_ref, mod_ref, lg_ref, lb_ref, rw_ref, rb_ref,
     h1_out, bin_out, ti_out, tw_out, cnt_out) = refs[2 * n_in:]
    i = pl.program_id(0)
    y = bias_ref[...]
    for x_ref, w_ref in zip(xs, ws):
        y = y + jnp.dot(x_ref[...], w_ref[...], preferred_element_type=F32)
    m = mod_ref[...]
    h1 = _layernorm(DEEPNORM_ALPHA * h_ref[...] + m[2:3] * y, lg_ref[...], lb_ref[...])
    h1_out[...] = h1
    b_in = (h1 * (1.0 + m[4:5]) + m[3:4]).astype(BF16)
    bin_out[...] = b_in
    lg = jnp.dot(b_in, rw_ref[...], preferred_element_type=F32) + rb_ref[...]
    lane = lax.broadcasted_iota(jnp.int32, lg.shape, 1)
    vals, idxs = [], []
    member = jnp.zeros(lg.shape, F32)
    for _ in range(TOP_K):
        mx = jnp.max(lg, axis=-1, keepdims=True)
        ix = jnp.min(jnp.where(lg == mx, lane, LANES), axis=-1, keepdims=True)
        hit = lane == ix
        member = member + hit.astype(F32)
        lg = jnp.where(hit, -jnp.inf, lg)
        vals.append(mx)
        idxs.append(ix)
    es = [jnp.exp(v - vals[0]) for v in vals]
    den = es[0] + es[1] + es[2] + es[3]
    ti = jnp.zeros(lg.shape, jnp.int32)
    tw = jnp.zeros(lg.shape, F32)
    for k in range(TOP_K):
        ti = jnp.where(lane == k, idxs[k], ti)
        tw = jnp.where(lane == k, es[k] / den, tw)
    ti_out[...] = ti
    tw_out[...] = tw

    @pl.when(i == 0)
    def _():
        cnt_out[...] = jnp.zeros(cnt_out.shape, F32)

    cnt_out[0:1, :] += jnp.sum(member, axis=0, keepdims=True)


def _post(xs, ws, bias, h3, h_map, mods, mod_map, ln_g, ln_b, rw, rb, n_tok):
    D = D_MODEL
    n_in = len(xs)
    const = lambda i: (0, 0)
    row = lambda i: (i, 0)
    in_specs = ([pl.BlockSpec((ROW_TILE, x.shape[1]), row) for x in xs]
                + [pl.BlockSpec(w.shape, const) for w in ws]
                + [pl.BlockSpec((1, D), const),
                   pl.BlockSpec((None, ROW_TILE, D), h_map),
                   pl.BlockSpec((None, 6, D), mod_map),
                   pl.BlockSpec((1, D), const),
                   pl.BlockSpec((1, D), const),
                   pl.BlockSpec((D, LANES), const),
                   pl.BlockSpec((1, LANES), const)])
    return pl.pallas_call(
        functools.partial(_post_kernel, n_in=n_in),
        grid=(n_tok // ROW_TILE,),
        in_specs=in_specs,
        out_specs=[pl.BlockSpec((ROW_TILE, D), row),
                   pl.BlockSpec((ROW_TILE, D), row),
                   pl.BlockSpec((ROW_TILE, LANES), row),
                   pl.BlockSpec((ROW_TILE, LANES), row),
                   pl.BlockSpec((8, LANES), const)],
        out_shape=[jax.ShapeDtypeStruct((n_tok, D), F32),
                   jax.ShapeDtypeStruct((n_tok, D), BF16),
                   jax.ShapeDtypeStruct((n_tok, LANES), jnp.int32),
                   jax.ShapeDtypeStruct((n_tok, LANES), F32),
                   jax.ShapeDtypeStruct((8, LANES), F32)],
        compiler_params=_params("arbitrary"),
        name="post_mixer",
    )(*xs, *ws, bias, h3, mods, ln_g, ln_b, rw, rb)


def _plan_kernel(ti_ref, ps_ref, lp_out, cnt_out, goff_out, carry_ref):
    i = pl.program_id(0)

    @pl.when(i == 0)
    def _():
        carry_ref[...] = jnp.zeros(carry_ref.shape, F32)

    ti = ti_ref[...]
    lane = lax.broadcasted_iota(jnp.int32, ti.shape, 1)
    hits = [lane == ti[:, k:k + 1] for k in range(TOP_K)]
    member = hits[0].astype(F32)
    for k in range(1, TOP_K):
        member = member + hits[k].astype(F32)
    row = lax.broadcasted_iota(jnp.int32, (ROW_TILE, ROW_TILE), 0)
    col = lax.broadcasted_iota(jnp.int32, (ROW_TILE, ROW_TILE), 1)
    before = (col < row).astype(BF16)
    rank = jnp.dot(before, member.astype(BF16), preferred_element_type=F32)
    cnt = jnp.sum(member, axis=0, keepdims=True)
    er = lax.broadcasted_iota(jnp.int32, (LANES, LANES), 0)
    ec = lax.broadcasted_iota(jnp.int32, (LANES, LANES), 1)
    cnt8 = jnp.broadcast_to(cnt, (8, LANES))
    loff = jnp.dot(cnt8.astype(BF16), (er < ec).astype(BF16), preferred_element_type=F32)[0:1, :]
    base = rank + loff
    lp = jnp.zeros(ti.shape, F32)
    for k in range(TOP_K):
        pk = jnp.sum(jnp.where(hits[k], base, 0.0), axis=-1, keepdims=True)
        lp = jnp.where(lane == k, pk, lp)
    lp_out[...] = lp.astype(jnp.int32)
    cnt_out[...] = cnt8.astype(jnp.int32)
    goff_out[...] = jnp.broadcast_to(carry_ref[0:1, :] + ps_ref[...], (8, LANES)).astype(jnp.int32)
    carry_ref[0:1, :] += cnt


def _plan(ti, pstart):
    n_tok = ti.shape[0]
    nt = n_tok // ROW_TILE
    return pl.pallas_call(
        _plan_kernel,
        grid=(nt,),
        in_specs=[pl.BlockSpec((ROW_TILE, LANES), lambda i: (i, 0)),
                  pl.BlockSpec((1, LANES), lambda i: (0, 0))],
        out_specs=[pl.BlockSpec((ROW_TILE, LANES), lambda i: (i, 0)),
                   pl.BlockSpec((None, 8, LANES), lambda i: (i, 0, 0)),
                   pl.BlockSpec((None, 8, LANES), lambda i: (i, 0, 0))],
        out_shape=[jax.ShapeDtypeStruct((n_tok, LANES), jnp.int32),
                   jax.ShapeDtypeStruct((nt, 8, LANES), jnp.int32),
                   jax.ShapeDtypeStruct((nt, 8, LANES), jnp.int32)],
        scratch_shapes=[pltpu.VMEM((8, LANES), F32)],
        compiler_params=_params("arbitrary"),
        name="moe_plan",
    )(ti, pstart)


SLOT_ROWS = ROW_TILE * TOP_K
RUN_BITS = ROW_TILE.bit_length()
ROW_SHAPE = (8, LANES)


def _row_copy(src, dst, sem):
    return pltpu.make_async_copy(src, dst, sem)


def _for_each_run_piece(cnt_ref, goff_ref, tile, fn):
    def body(e, lo):
        c = cnt_ref[tile * N_EXPERTS + e]
        g = goff_ref[tile * N_EXPERTS + e]
        for b in range(RUN_BITS):
            size = 1 << b

            @pl.when(jnp.bitwise_and(lax.shift_right_logical(c, b), 1) == 1)
            def _():
                off = jnp.bitwise_and(c, size - 1)
                fn(lo + off, g + off, size)
        return lo + c

    lax.fori_loop(0, N_EXPERTS, body, 0)


def _slot_onehot(lp, n_cols):
    pos = lax.broadcasted_iota(jnp.int32, (lp.shape[0], n_cols), 1)
    hit = pos == lp[:, 0:1]
    for k in range(1, TOP_K):
        hit = jnp.logical_or(hit, pos == lp[:, k:k + 1])
    return hit


def _dispatch_kernel(bv_ref, cnt_ref, goff_ref, lp_ref, x_ref, xs_hbm, zero_ref, buf_ref, sem, zsem):
    n_blocks = xs_hbm.shape[0] // MOE_TILE
    i = pl.program_id(0)
    nt = pl.num_programs(0)
    slot = i % 2

    def wait_slot(s):
        _row_copy(buf_ref.at[s], xs_hbm.at[pl.ds(0, SLOT_ROWS)], sem.at[s]).wait()

    @pl.when(pl.program_id(0) == 0)
    def _():
        zero_ref[...] = jnp.zeros(zero_ref.shape, F32)

        def fill(wait):
            def body(j, carry):
                @pl.when(bv_ref[j] < MOE_TILE)
                def _():
                    rows = pl.ds(pl.multiple_of(j * MOE_TILE, MOE_TILE), MOE_TILE)
                    cp = _row_copy(zero_ref, xs_hbm.at[rows], zsem)
                    if wait:
                        cp.wait()
                    else:
                        cp.start()
                return carry
            return body

        lax.fori_loop(0, n_blocks, fill(False), 0)
        lax.fori_loop(0, n_blocks, fill(True), 0)

    @pl.when(i >= 2)
    def _():
        wait_slot(slot)

    onehot = _slot_onehot(lp_ref[...], SLOT_ROWS).astype(BF16)
    rows = lax.dot_general(onehot, x_ref[...], (((0,), (0,)), ((), ())), preferred_element_type=F32)
    buf_ref[slot] = rows.reshape((SLOT_ROWS,) + ROW_SHAPE)

    def copy_out(lo, g, size):
        _row_copy(buf_ref.at[slot, pl.ds(lo, size)], xs_hbm.at[pl.ds(g, size)], sem.at[slot]).start()

    _for_each_run_piece(cnt_ref, goff_ref, i, copy_out)

    @pl.when(i == nt - 1)
    def _():
        wait_slot(slot)

        @pl.when(nt >= 2)
        def _():
            wait_slot(1 - slot)


def _dispatch(block_valid, cnt_flat, goff_flat, lp, b_in, n_rows):
    n_tok, D = b_in.shape
    grid_spec = pltpu.PrefetchScalarGridSpec(
        num_scalar_prefetch=3,
        grid=(n_tok // ROW_TILE,),
        in_specs=[pl.BlockSpec((ROW_TILE, LANES), lambda i, *_: (i, 0)),
                  pl.BlockSpec((ROW_TILE, D), lambda i, *_: (i, 0))],
        out_specs=pl.BlockSpec(memory_space=pl.ANY),
        scratch_shapes=[pltpu.VMEM((MOE_TILE,) + ROW_SHAPE, F32), pltpu.VMEM((2, SLOT_ROWS) + ROW_SHAPE, F32),
                        pltpu.SemaphoreType.DMA((2,)), pltpu.SemaphoreType.DMA(())],
    )
    return pl.pallas_call(
        _dispatch_kernel,
        grid_spec=grid_spec,
        out_shape=jax.ShapeDtypeStruct((n_rows,) + ROW_SHAPE, F32),
        compiler_params=_params("arbitrary"),
        name="moe_dispatch",
    )(block_valid, cnt_flat, goff_flat, lp, b_in)


GU_GROUP = 2 * LANES


def _pair_split_matrix():
    r = np.arange(GU_GROUP)
    p = np.zeros((GU_GROUP, GU_GROUP), np.float32)
    p[r, np.where(r % 2 == 0, r // 2, LANES + r // 2)] = 1.0
    return jnp.asarray(p, BF16)


def _expert_kernel(be_ref, bv_ref, x_ref, wgu_ref, bgu_ref, wd_ref, bd_ref, split_ref, y_ref, wgu_s, wd_s):
    j = pl.program_id(0)
    valid = bv_ref[j]
    n_groups = wgu_s.shape[1] // GU_GROUP
    changed = jnp.logical_or(j == 0, be_ref[j] != be_ref[jnp.maximum(j - 1, 0)])

    @pl.when(changed)
    def _():
        split = split_ref[...]
        for g in range(n_groups):
            cols = slice(g * GU_GROUP, (g + 1) * GU_GROUP)
            wgu_s[:, cols] = jnp.dot(wgu_ref[:, cols].astype(BF16), split,
                                     preferred_element_type=F32).astype(BF16)
        wd_s[...] = wd_ref[...].astype(BF16)

    @pl.when(valid > 0)
    def _():
        x = x_ref[...].reshape(MOE_TILE, wd_s.shape[1])
        rowi = lax.broadcasted_iota(jnp.int32, x.shape, 0)
        x = jnp.where(rowi < valid, x, 0.0).astype(BF16)
        gu = jnp.dot(x, wgu_s[...], preferred_element_type=F32) + bgu_ref[...]
        acts = []
        for g in range(n_groups):
            gate = jnp.minimum(gu[:, g * GU_GROUP:g * GU_GROUP + LANES], SWIGLU_LIMIT)
            up = jnp.clip(gu[:, g * GU_GROUP + LANES:(g + 1) * GU_GROUP], -SWIGLU_LIMIT, SWIGLU_LIMIT)
            acts.append(((up + 1.0) * gate * _sigmoid(SWIGLU_ALPHA * gate)).astype(BF16))
        act = jnp.concatenate(acts, axis=1)
        y = jnp.dot(act, wd_s[...], preferred_element_type=F32) + bd_ref[...]
        y_ref[...] = y.reshape(y_ref.shape)

    @pl.when(valid <= 0)
    def _():
        y_ref[...] = jnp.zeros(y_ref.shape, F32)


def _experts(block_e, block_valid, xs, layer, w_gu, b_gu_split, w_down, b_down):
    P = xs.shape[0]
    _, E, D, W = w_gu.shape
    DE = w_down.shape[2]
    wmap = lambda j, be, bv: (layer, be[j], 0, 0)
    grid_spec = pltpu.PrefetchScalarGridSpec(
        num_scalar_prefetch=2,
        grid=(P // MOE_TILE,),
        in_specs=[pl.BlockSpec((MOE_TILE,) + ROW_SHAPE, lambda j, be, bv: (j, 0, 0)),
                  pl.BlockSpec((None, None, D, W), wmap),
                  pl.BlockSpec((None, None, 1, W), wmap),
                  pl.BlockSpec((None, None, DE, D), wmap),
                  pl.BlockSpec((None, None, 1, D), wmap),
                  pl.BlockSpec((GU_GROUP, GU_GROUP), lambda j, be, bv: (0, 0))],
        out_specs=pl.BlockSpec((MOE_TILE,) + ROW_SHAPE, lambda j, be, bv: (j, 0, 0)),
        scratch_shapes=[pltpu.VMEM((D, W), BF16), pltpu.VMEM((DE, D), BF16)],
    )
    return pl.pallas_call(
        _expert_kernel,
        grid_spec=grid_spec,
        out_shape=jax.ShapeDtypeStruct((P,) + ROW_SHAPE, F32),
        compiler_params=_params("arbitrary"),
        name="moe_experts",
    )(block_e, block_valid, xs, w_gu, b_gu_split, w_down, b_down, _pair_split_matrix())


def _combine_kernel(cnt_ref, goff_ref, ys_hbm, lp_ref, tw_ref, h_ref, mod_ref, lg_ref, lb_ref, o_ref,
                    buf_ref, sem):
    i = pl.program_id(0)
    nt = pl.num_programs(0)
    slot = i % 2

    def fetch(tile, s):
        def copy_in(lo, g, size):
            _row_copy(ys_hbm.at[pl.ds(g, size)], buf_ref.at[s, pl.ds(lo, size)], sem.at[s]).start()

        _for_each_run_piece(cnt_ref, goff_ref, tile, copy_in)

    @pl.when(i == 0)
    def _():
        fetch(0, 0)

    @pl.when(i + 1 < nt)
    def _():
        fetch(i + 1, 1 - slot)

    _row_copy(ys_hbm.at[pl.ds(0, SLOT_ROWS)], buf_ref.at[slot], sem.at[slot]).wait()

    lp = lp_ref[...]
    tw = tw_ref[...]
    pos = lax.broadcasted_iota(jnp.int32, (ROW_TILE, SLOT_ROWS), 1)
    wsel = jnp.where(pos == lp[:, 0:1], tw[:, 0:1], 0.0)
    for k in range(1, TOP_K):
        wsel = wsel + jnp.where(pos == lp[:, k:k + 1], tw[:, k:k + 1], 0.0)
    w_hi = wsel.astype(BF16)
    w_lo = (wsel - w_hi.astype(F32)).astype(BF16)
    ys = buf_ref[slot].reshape(SLOT_ROWS, h_ref.shape[1]).astype(BF16)
    f = (jnp.dot(w_hi, ys, preferred_element_type=F32) + jnp.dot(w_lo, ys, preferred_element_type=F32))
    m = mod_ref[...]
    o_ref[...] = _layernorm(DEEPNORM_ALPHA * h_ref[...] + m[5:6] * f, lg_ref[...], lb_ref[...])


def _combine(cnt_flat, goff_flat, ys, lp, tw, h1, mods, mod_map, ln_g, ln_b):
    n_tok, D = h1.shape
    const = lambda i, *_: (0, 0)
    row = lambda i, *_: (i, 0)
    grid_spec = pltpu.PrefetchScalarGridSpec(
        num_scalar_prefetch=2,
        grid=(n_tok // ROW_TILE,),
        in_specs=[pl.BlockSpec(memory_space=pl.ANY),
                  pl.BlockSpec((ROW_TILE, LANES), row),
                  pl.BlockSpec((ROW_TILE, LANES), row),
                  pl.BlockSpec((ROW_TILE, D), row),
                  pl.BlockSpec((None, 6, D), lambda i, *_: mod_map(i)),
                  pl.BlockSpec((1, D), const),
                  pl.BlockSpec((1, D), const)],
        out_specs=pl.BlockSpec((ROW_TILE, D), row),
        scratch_shapes=[pltpu.VMEM((2, SLOT_ROWS) + ROW_SHAPE, F32), pltpu.SemaphoreType.DMA((2,))],
    )
    return pl.pallas_call(
        _combine_kernel,
        grid_spec=grid_spec,
        out_shape=jax.ShapeDtypeStruct((n_tok, D), F32),
        compiler_params=_params("arbitrary"),
        name="moe_combine",
    )(cnt_flat, goff_flat, ys, lp, tw, h1, mods, ln_g, ln_b)


def _moe(b_in, ti, tw, counts, h1, mods, mod_map, ln_g, ln_b, layer, w_gu, b_gu, w_down, b_down):
    n_tok = b_in.shape[0]
    cnt = counts[0, :N_EXPERTS].astype(jnp.int32)
    pcnt = (cnt + MOE_TILE - 1) // MOE_TILE * MOE_TILE
    pends = jnp.cumsum(pcnt)
    pstart = pends - pcnt
    n_blocks = -(-(n_tok * TOP_K + N_EXPERTS * (MOE_TILE - 1)) // MOE_TILE)
    blk0 = jnp.arange(n_blocks, dtype=jnp.int32) * MOE_TILE
    block_e = jnp.minimum(jnp.sum((pends[None, :] <= blk0[:, None]).astype(jnp.int32), axis=1), N_EXPERTS - 1)
    block_valid = jnp.clip(cnt[block_e] - (blk0 - pstart[block_e]), 0, MOE_TILE).astype(jnp.int32)
    ps_row = jnp.zeros((1, LANES), F32).at[0, :N_EXPERTS].set(pstart.astype(F32))
    lp, cnt_t, goff_t = _plan(ti, ps_row)
    cnt_flat = cnt_t[:, 0, :N_EXPERTS].reshape(-1)
    goff_flat = goff_t[:, 0, :N_EXPERTS].reshape(-1)
    xs = _dispatch(block_valid, cnt_flat, goff_flat, lp, b_in, n_blocks * MOE_TILE)
    L, E, W = b_gu.shape
    b_split = b_gu.reshape(L, E, W // GU_GROUP, LANES, 2).transpose(0, 1, 2, 4, 3).reshape(L, E, 1, W)
    ys = _experts(block_e, block_valid, xs, layer, w_gu, b_split, w_down, b_down.reshape(L, E, 1, -1))
    return _combine(cnt_flat, goff_flat, ys, lp, tw, h1, mods, mod_map, ln_g, ln_b)


def _router_weights(router_w, router_b):
    D = router_w.shape[0]
    rw = jnp.zeros((D, LANES), F32).at[:, :N_EXPERTS].set(router_w).astype(BF16)
    rb = jnp.full((1, LANES), -jnp.inf, F32).at[0, :N_EXPERTS].set(router_b)
    return rw, rb


def _glu_kernel(h_ref, mod_ref, w_ref, b_ref, u_out):
    m = mod_ref[...]
    a = (h_ref[...] * (1.0 + m[1:2]) + m[0:1]).astype(BF16)
    z = jnp.dot(a, w_ref[...], preferred_element_type=F32) + b_ref[...]
    half = z.shape[1] // 2
    u_out[...] = z[:, :half] * _sigmoid(z[:, half:])


def _glu(h3, mods, w, b, B, tpb_lat):
    D = D_MODEL
    return pl.pallas_call(
        _glu_kernel,
        grid=(B, tpb_lat),
        in_specs=[pl.BlockSpec((None, ROW_TILE, D), lambda b, j: (b, j + 1, 0)),
                  pl.BlockSpec((None, 6, D), lambda b, j: (b, 0, 0)),
                  pl.BlockSpec(w.shape, lambda b, j: (0, 0)),
                  pl.BlockSpec((1, w.shape[1]), lambda b, j: (0, 0))],
        out_specs=pl.BlockSpec((None, ROW_TILE, D), lambda b, j: (b, j, 0)),
        out_shape=jax.ShapeDtypeStruct((B, tpb_lat * ROW_TILE, D), F32),
        compiler_params=_params("arbitrary", "arbitrary"),
        name="conv_glu",
    )(h3, mods, w, b)


def _dwconv_kernel(prev_ref, cur_ref, next_ref, w_ref, b_ref, lg_ref, lb_ref, o_ref, win_ref, *, tpb_lat):
    j = pl.program_id(1)
    H = CONV_HALO
    zero = jnp.zeros((H, cur_ref.shape[1]), F32)
    win_ref[0:H, :] = jnp.where(j > 0, prev_ref[...], zero)
    win_ref[H:H + ROW_TILE, :] = cur_ref[...]
    win_ref[H + ROW_TILE:, :] = jnp.where(j < tpb_lat - 1, next_ref[...], zero)
    w = w_ref[...]
    off = H - CONV_WIDTH // 2
    acc = jnp.zeros(cur_ref.shape, F32) + b_ref[...]
    for k in range(CONV_WIDTH):
        acc = acc + win_ref[off + k:off + k + ROW_TILE, :] * w[k:k + 1, :]
    y = _layernorm(acc, lg_ref[...], lb_ref[...])
    o_ref[...] = (y * _sigmoid(y)).astype(o_ref.dtype)


def _dwconv(u, w, b, ln_g, ln_b, B, tpb_lat):
    D = D_MODEL
    n = tpb_lat * ROW_TILE
    per = ROW_TILE // CONV_HALO
    last = n // CONV_HALO - 1
    const = lambda b_, j: (0, 0)
    return pl.pallas_call(
        functools.partial(_dwconv_kernel, tpb_lat=tpb_lat),
        grid=(B, tpb_lat),
        in_specs=[pl.BlockSpec((None, CONV_HALO, D), lambda b_, j: (b_, jnp.maximum(j * per - 1, 0), 0)),
                  pl.BlockSpec((None, ROW_TILE, D), lambda b_, j: (b_, j, 0)),
                  pl.BlockSpec((None, CONV_HALO, D), lambda b_, j: (b_, jnp.minimum((j + 1) * per, last), 0)),
                  pl.BlockSpec(w.shape, const),
                  pl.BlockSpec((1, D), const),
                  pl.BlockSpec((1, D), const),
                  pl.BlockSpec((1, D), const)],
        out_specs=pl.BlockSpec((ROW_TILE, D), lambda b_, j: (b_ * tpb_lat + j, 0)),
        out_shape=jax.ShapeDtypeStruct((B * n, D), BF16),
        scratch_shapes=[pltpu.VMEM((ROW_TILE + 2 * CONV_HALO, D), F32)],
        compiler_params=_params("arbitrary", "arbitrary"),
        name="conv_dw",
    )(u, u, u, w, b, ln_g, ln_b)


def kernel(x, c, ctx, c_ctx, mod_w, mod_b, ln1_g, ln1_b, ln2_g, ln2_b, mix_w_in, mla_q_norm, mla_kv_norm, mla_w_uq, mla_w_uk, mla_w_uv, hg_lb_logits, hg_norm, mix_w_out, conv_pw1_w, conv_pw1_b, conv_dw_w, conv_dw_b, conv_ln_g, conv_ln_b, conv_pw2_w, conv_pw2_b, router_w, router_b, exp_w_gu, exp_b_gu, exp_w_down, exp_b_down):
    B, n, D = x.shape
    tc = ctx.shape[1]
    assert D == D_MODEL and tc == ROW_TILE and n % ROW_TILE == 0 and n % GRID_W == 0
    assert B + 1 <= 16
    T = tc + n
    tpb = T // ROW_TILE
    tpb_lat = n // ROW_TILE
    N = B * T
    row2 = lambda v: v.reshape(1, -1)

    cond = jnp.zeros((16, D), F32).at[:B].set(c).at[B].set(c_ctx)
    mods = _modulation(cond, mod_w, mod_b)[:, :B + 1].reshape(DEPTH, B + 1, 6, D)
    lb_all = jnp.cumsum(jax.nn.softmax(hg_lb_logits.astype(F32), axis=0), axis=0)

    h0 = jnp.concatenate([ctx, x], axis=1)
    win, wq2, wkv = _inproj_weights(mix_w_in[0], mla_w_uq[0], mla_w_uk[0], mla_w_uv[0])
    cos_t, sin_t = _rope_tables(n, tc)
    q, k, v, hg = _inproj(h0.reshape(N, D), mods[0], win, row2(mla_q_norm[0]), row2(mla_kv_norm[0]),
                          wq2, wkv, cos_t, sin_t, B, tpb)
    att = _attention(q, k, v, B, tpb, tc)
    hgo = _hgrn(hg, lb_all[0].reshape(HG_HEADS, 1, HG_DK), row2(hg_norm[0]), B, tpb)
    w_out = mix_w_out[0].astype(BF16)
    n_att = MLA_HEADS * MLA_V
    rw, rb = _router_weights(router_w[0], router_b[0])
    mod_map0 = lambda i: (jnp.where(i % tpb == 0, B, i // tpb), 0, 0)
    h1, b_in, ti, tw, counts = _post(
        [att, hgo], [w_out[:n_att], w_out[n_att:]], jnp.zeros((1, D), F32),
        h0.reshape(N // ROW_TILE, ROW_TILE, D), lambda i: (i, 0, 0), mods[0], mod_map0,
        row2(ln1_g[0]), row2(ln1_b[0]), rw, rb, N)
    ew = (exp_w_gu, exp_b_gu, exp_w_down, exp_b_down)
    h2 = _moe(b_in, ti, tw, counts, h1, mods[0], mod_map0, row2(ln2_g[0]), row2(ln2_b[0]), 0, *ew)

    n_lat = B * n
    h2_3 = h2.reshape(B * tpb, ROW_TILE, D)
    u = _glu(h2.reshape(B, T, D), mods[1], conv_pw1_w[0].astype(BF16), row2(conv_pw1_b[0]), B, tpb_lat)
    cv = _dwconv(u, conv_dw_w[0], row2(conv_dw_b[0]), row2(conv_ln_g[0]), row2(conv_ln_b[0]), B, tpb_lat)
    rw, rb = _router_weights(router_w[1], router_b[1])
    mod_map1 = lambda i: (i // tpb_lat, 0, 0)
    h_map1 = lambda i: ((i // tpb_lat) * tpb + i % tpb_lat + 1, 0, 0)
    h3, b_in, ti, tw, counts = _post(
        [cv], [conv_pw2_w[0].astype(BF16)], row2(conv_pw2_b[0]),
        h2_3, h_map1, mods[1], mod_map1, row2(ln1_g[1]), row2(ln1_b[1]), rw, rb, n_lat)
    h4 = _moe(b_in, ti, tw, counts, h3, mods[1], mod_map1, row2(ln2_g[1]), row2(ln2_b[1]), 1, *ew)
    return h4.reshape(B, n, D)
```

```python
import functools

import numpy as np
import jax
import jax.numpy as jnp
from jax import lax
from jax.experimental import pallas as pl
from jax.experimental.pallas import tpu as pltpu

F32 = jnp.float32
BF16 = jnp.bfloat16

D_MODEL = 1024
DEPTH = 2
GRID_W = 64
MLA_HEADS = 8
MLA_Q_LORA = 384
MLA_KV_LORA = 256
MLA_NOPE = 64
MLA_ROPE = 32
MLA_V = 64
MLA_SCALE = (MLA_NOPE + MLA_ROPE) ** -0.5
LOG2_E = 1.4426950408889634
ROPE_BASE = 10000.0
HG_HEADS = 4
HG_DK = 128
HG_CHUNK = 32
HG_K = HG_HEADS * HG_DK
CONV_WIDTH = 31
N_EXPERTS = 32
TOP_K = 4
SWIGLU_LIMIT = 7.0
SWIGLU_ALPHA = 1.702
LN_EPS = 1e-5
RMS_EPS = 1e-6
DEEPNORM_ALPHA = (2.0 * DEPTH) ** 0.25

LANES = 128
ROW_TILE = 256
HEAD_BLOCK = 128
KEY_CHUNK = 1024
MOE_TILE = 512
EXPERT_SUB = 512
CONV_HALO = 16
IN_WIDTH = 3456
HG_PARTS = 5 * HG_K
VMEM_LIMIT = 56 * 1024 * 1024


def _params(*sem):
    return pltpu.CompilerParams(dimension_semantics=sem, vmem_limit_bytes=VMEM_LIMIT)


def _sigmoid(x):
    return 1.0 / (1.0 + jnp.exp(-x))


def _layernorm(x, g, b):
    xc = x - jnp.mean(x, axis=-1, keepdims=True)
    var = jnp.mean(xc * xc, axis=-1, keepdims=True)
    return xc * lax.rsqrt(var + LN_EPS) * g + b


def _rms(x, g):
    return x * lax.rsqrt(jnp.mean(x * x, axis=-1, keepdims=True) + RMS_EPS) * g


def _mod_kernel(c_ref, w_ref, b_ref, o_ref):
    c = c_ref[...]
    s = (c * _sigmoid(c)).astype(BF16)
    o_ref[...] = jnp.dot(s, w_ref[...].astype(BF16), preferred_element_type=F32) + b_ref[...]


def _modulation(cond, mod_w, mod_b):
    L, D, W = mod_w.shape
    R = cond.shape[0]
    return pl.pallas_call(
        _mod_kernel,
        grid=(L, W // D),
        in_specs=[pl.BlockSpec((R, D), lambda l, n: (0, 0)),
                  pl.BlockSpec((None, D, D), lambda l, n: (l, 0, n)),
                  pl.BlockSpec((None, 1, D), lambda l, n: (l, 0, n))],
        out_specs=pl.BlockSpec((None, R, D), lambda l, n: (l, 0, n)),
        out_shape=jax.ShapeDtypeStruct((L, R, W), F32),
        compiler_params=_params("arbitrary", "arbitrary"),
        name="adaln_mod",
    )(cond, mod_w, mod_b.reshape(L, 1, W))


def _inproj_kernel(h_ref, mod_ref, win_ref, qn_ref, kvn_ref, wq_ref, wkv_ref, c_ref, s_ref,
                   q_out, k_out, v_out, hg_out):
    m = mod_ref[...]
    a = (h_ref[...] * (1.0 + m[1:2]) + m[0:1]).astype(BF16)
    z = jnp.dot(a, win_ref[...], preferred_element_type=F32)
    cos = c_ref[...]
    sin = s_ref[...]
    cos_h = jnp.concatenate([cos] * MLA_HEADS, axis=1)
    sin_h = jnp.concatenate([sin] * MLA_HEADS, axis=1)
    width = MLA_HEADS * HEAD_BLOCK

    cqn = _rms(z[:, :MLA_Q_LORA], qn_ref[...]).astype(BF16)
    qq = jnp.dot(cqn, wq_ref[...], preferred_element_type=F32)
    q_out[...] = ((qq[:, :width] * cos_h + qq[:, width:] * sin_h) * (MLA_SCALE * LOG2_E)).astype(BF16)

    c0 = MLA_Q_LORA + MLA_KV_LORA
    ckvn = _rms(z[:, MLA_Q_LORA:c0], kvn_ref[...]).astype(BF16)
    kv = jnp.dot(ckvn, wkv_ref[...], preferred_element_type=F32)
    r0 = c0 + HG_PARTS
    kr = z[:, r0:r0 + HEAD_BLOCK] * cos + z[:, r0 + HEAD_BLOCK:r0 + 2 * HEAD_BLOCK] * sin
    k_out[...] = (kv[:, :width] + jnp.concatenate([kr] * MLA_HEADS, axis=1)).astype(BF16)
    v_out[...] = kv[:, width:].astype(BF16)
    hg_out[...] = z[:, c0:r0]


def _rope_partner(w):
    shp = w.shape
    wr = w.reshape(shp[:-1] + (shp[-1] // 2, 2))
    return jnp.stack([-wr[..., 1], wr[..., 0]], axis=-1).reshape(shp)


def _inproj_weights(w_in, w_uq, w_uk, w_uv):
    D = w_in.shape[0]
    o_kr = MLA_Q_LORA + MLA_KV_LORA
    w_kr = w_in[:, o_kr:o_kr + MLA_ROPE]
    z_nope = jnp.zeros((D, MLA_NOPE), F32)
    z_pad = jnp.zeros((D, HEAD_BLOCK - MLA_NOPE - MLA_ROPE), F32)
    krb = jnp.concatenate([z_nope, w_kr, z_pad], axis=1)
    krb_p = jnp.concatenate([z_nope, _rope_partner(w_kr), z_pad], axis=1)
    win = jnp.concatenate([w_in[:, :o_kr], w_in[:, o_kr + MLA_ROPE:], krb, krb_p], axis=1).astype(BF16)

    R = w_uq.shape[0]
    wq = w_uq.reshape(R, MLA_HEADS, MLA_NOPE + MLA_ROPE)
    nope, rope = wq[..., :MLA_NOPE], wq[..., MLA_NOPE:]
    zp = jnp.zeros((R, MLA_HEADS, HEAD_BLOCK - MLA_NOPE - MLA_ROPE), F32)
    zn = jnp.zeros((R, MLA_HEADS, MLA_NOPE), F32)
    q_main = jnp.concatenate([nope, rope, zp], axis=-1).reshape(R, MLA_HEADS * HEAD_BLOCK)
    q_part = jnp.concatenate([zn, _rope_partner(rope), zp], axis=-1).reshape(R, MLA_HEADS * HEAD_BLOCK)
    wq2 = jnp.concatenate([q_main, q_part], axis=1).astype(BF16)

    C = w_uk.shape[0]
    wk = w_uk.reshape(C, MLA_HEADS, MLA_NOPE)
    wk = jnp.concatenate([wk, jnp.zeros((C, MLA_HEADS, HEAD_BLOCK - MLA_NOPE), F32)], axis=-1)
    wv = w_uv.reshape(C, MLA_HEADS // 2, 2, MLA_V)
    zv = jnp.zeros((C, MLA_HEADS // 2, MLA_V), F32)
    wv = jnp.stack([jnp.concatenate([wv[:, :, 0], zv], -1), jnp.concatenate([zv, wv[:, :, 1]], -1)], axis=2)
    wkv = jnp.concatenate([wk.reshape(C, -1), wv.reshape(C, -1)], axis=1).astype(BF16)
    return win, wq2, wkv


def _rope_tables(n, tc):
    rows = n // GRID_W
    pos_r = jnp.repeat(jnp.arange(rows, dtype=F32), GRID_W)
    pos_c = jnp.tile(jnp.arange(GRID_W, dtype=F32), rows)
    half = MLA_ROPE // 2
    inv = jnp.power(ROPE_BASE, -jnp.arange(0, half, 2, dtype=F32) / half)
    ang = jnp.concatenate([pos_r[:, None] * inv, pos_c[:, None] * inv], axis=-1)
    cos2 = jnp.repeat(jnp.cos(ang), 2, axis=-1)
    sin2 = jnp.repeat(jnp.sin(ang), 2, axis=-1)
    pad = HEAD_BLOCK - MLA_NOPE - MLA_ROPE
    cos_l = jnp.concatenate([jnp.ones((n, MLA_NOPE), F32), cos2, jnp.ones((n, pad), F32)], axis=1)
    sin_l = jnp.concatenate([jnp.zeros((n, MLA_NOPE), F32), sin2, jnp.zeros((n, pad), F32)], axis=1)
    cos_t = jnp.concatenate([jnp.ones((tc, HEAD_BLOCK), F32), cos_l], axis=0)
    sin_t = jnp.concatenate([jnp.zeros((tc, HEAD_BLOCK), F32), sin_l], axis=0)
    return cos_t, sin_t


def _inproj(h, mods, win, qn, kvn, wq2, wkv, cos_t, sin_t, B, tpb):
    N, D = h.shape
    ctx_row = mods.shape[0] - 1
    width = MLA_HEADS * HEAD_BLOCK
    const = lambda i: (0, 0)
    return pl.pallas_call(
        _inproj_kernel,
        grid=(N // ROW_TILE,),
        in_specs=[pl.BlockSpec((ROW_TILE, D), lambda i: (i, 0)),
                  pl.BlockSpec((None, 6, D), lambda i: (jnp.where(i % tpb == 0, ctx_row, i // tpb), 0, 0)),
                  pl.BlockSpec(win.shape, const),
                  pl.BlockSpec(qn.shape, const),
                  pl.BlockSpec(kvn.shape, const),
                  pl.BlockSpec(wq2.shape, const),
                  pl.BlockSpec(wkv.shape, const),
                  pl.BlockSpec((ROW_TILE, HEAD_BLOCK), lambda i: (i % tpb, 0)),
                  pl.BlockSpec((ROW_TILE, HEAD_BLOCK), lambda i: (i % tpb, 0))],
        out_specs=[pl.BlockSpec((ROW_TILE, width), lambda i: (i, 0)),
                   pl.BlockSpec((ROW_TILE, width), lambda i: (i, 0)),
                   pl.BlockSpec((ROW_TILE, width), lambda i: (i, 0)),
                   pl.BlockSpec((ROW_TILE, HG_PARTS), lambda i: (i, 0))],
        out_shape=[jax.ShapeDtypeStruct((N, width), BF16),
                   jax.ShapeDtypeStruct((N, width), BF16),
                   jax.ShapeDtypeStruct((N, width), BF16),
                   jax.ShapeDtypeStruct((N, HG_PARTS), F32)],
        compiler_params=_params("arbitrary"),
        name="inproj",
    )(h, mods, win, qn, kvn, wq2, wkv, cos_t, sin_t)


ATTN_HEADS = 4


def _attn_kernel(q_ref, k_ref, v_ref, o_ref, *, chunks):
    j = pl.program_id(2)

    def scores(hh, n_chunks):
        lo, hi = hh * HEAD_BLOCK, (hh + 1) * HEAD_BLOCK
        q = q_ref[:, lo:hi]
        return [lax.dot_general(q, k_ref[s0:s1, lo:hi], (((1,), (1,)), ((), ())), preferred_element_type=F32)
                for s0, s1 in chunks[:n_chunks]]

    def softmax_values(hh, ss, n_chunks):
        lo, hi = hh * HEAD_BLOCK, (hh + 1) * HEAD_BLOCK
        m = jnp.max(ss[0], axis=-1, keepdims=True)
        for s in ss[1:]:
            m = jnp.maximum(m, jnp.max(s, axis=-1, keepdims=True))
        l = acc = None
        for s, (s0, s1) in zip(ss, chunks[:n_chunks]):
            p = jnp.exp2(s - m)
            lc = jnp.sum(p, axis=-1, keepdims=True)
            ac = jnp.dot(p.astype(BF16), v_ref[s0:s1, lo:hi], preferred_element_type=F32)
            l = lc if l is None else l + lc
            acc = ac if acc is None else acc + ac
        return acc / l

    def pairs(n_chunks):
        heads = [None] * ATTN_HEADS
        ss = scores(0, n_chunks)
        for hh in range(ATTN_HEADS):
            nxt = scores(hh + 1, n_chunks) if hh + 1 < ATTN_HEADS else None
            heads[hh] = softmax_values(hh, ss, n_chunks)
            ss = nxt
        outs = [heads[hh] + heads[hh + 1] for hh in range(0, ATTN_HEADS, 2)]
        o_ref[...] = jnp.concatenate(outs, axis=1).astype(o_ref.dtype)

    @pl.when(j == 0)
    def _():
        pairs(1)

    @pl.when(j > 0)
    def _():
        pairs(len(chunks))


def _attention(q, k, v, B, tpb, tc):
    N = q.shape[0]
    T = tpb * ROW_TILE
    kc = min(KEY_CHUNK, T - tc)
    chunks = ((0, tc),) + tuple((s, s + kc) for s in range(tc, T, kc))
    width = ATTN_HEADS * HEAD_BLOCK
    return pl.pallas_call(
        functools.partial(_attn_kernel, chunks=chunks),
        grid=(B, MLA_HEADS // ATTN_HEADS, tpb),
        in_specs=[pl.BlockSpec((ROW_TILE, width), lambda b, p, j: (b * tpb + j, p)),
                  pl.BlockSpec((T, width), lambda b, p, j: (b, p)),
                  pl.BlockSpec((T, width), lambda b, p, j: (b, p))],
        out_specs=pl.BlockSpec((ROW_TILE, ATTN_HEADS * MLA_V), lambda b, p, j: (b * tpb + j, p)),
        out_shape=jax.ShapeDtypeStruct((N, MLA_HEADS * MLA_V), BF16),
        compiler_params=_params("arbitrary", "arbitrary", "arbitrary"),
        name="mla_attention",
    )(q, k, v)


def _hgrn_kernel(q_ref, ff_ref, fb_ref, v_ref, g_ref, lb_ref, gn_ref, o_ref, oacc_ref, obwd_ref, *, tpb):
    lb = lb_ref[...]
    gn = gn_ref[...]
    C = HG_CHUNK
    per_tile = ROW_TILE // C
    row = lax.broadcasted_iota(jnp.int32, (ROW_TILE, ROW_TILE), 0)
    col = lax.broadcasted_iota(jnp.int32, (ROW_TILE, ROW_TILE), 1)
    same = (row // C) == (col // C)
    ones_blk = same.astype(BF16)
    row_chunk = lax.broadcasted_iota(jnp.int32, (ROW_TILE, HG_DK), 0) // C
    nt_dims = (((1,), (1,)), ((), ()))
    tn_dims = (((0,), (0,)), ((), ()))

    def tile_rows(t):
        return pl.ds(pl.multiple_of(t * ROW_TILE, ROW_TILE), ROW_TILE)

    def chunk_sums(m01, g3):
        return sum(jnp.dot(m01, part, preferred_element_type=F32) for part in g3)

    def per_chunk_blocks(x):
        zero = jnp.zeros(x.shape, x.dtype)
        return jnp.concatenate([jnp.where(row_chunk == c, x, zero) for c in range(per_tile)], axis=1)

    def tile_step(t, st, f_ref, reverse, acc_ref):
        rows = tile_rows(t)
        f = lb + (1.0 - lb) * _sigmoid(f_ref[rows, :])
        kk = 1.0 - f
        g = jnp.log(f)
        g_hi = g.astype(BF16)
        r1 = g - g_hi.astype(F32)
        g_mid = r1.astype(BF16)
        g3 = (g_hi, g_mid, (r1 - g_mid.astype(F32)).astype(BF16))
        in_chunk = same & ((col >= row) if reverse else (col <= row))
        bcum = chunk_sums(in_chunk.astype(BF16), g3)
        btot = chunk_sums(ones_blk, g3)
        yield
        qin = (q_ref[rows, :] * jnp.exp(bcum)).astype(BF16)
        kdec = (kk * jnp.exp(-bcum)).astype(BF16)
        krem = (kk * jnp.exp(btot - bcum)).astype(BF16)
        dec = jnp.exp(btot)
        v = v_ref[rows, :].astype(BF16)
        a = lax.dot_general(qin, kdec, nt_dims, preferred_element_type=F32)
        ut = lax.dot_general(v, per_chunk_blocks(krem), tn_dims, preferred_element_type=F32)
        yield
        a = jnp.where(in_chunk, a, 0.0).astype(BF16)
        o_t = jnp.dot(a, v, preferred_element_type=F32)
        states = [None] * per_tile
        for c in (range(per_tile - 1, -1, -1) if reverse else range(per_tile)):
            states[c] = st.astype(BF16)
            st = st * dec[c * C:c * C + 1, :] + ut[:, c * HG_DK:(c + 1) * HG_DK]
        yield
        o_t = o_t + lax.dot_general(per_chunk_blocks(qin), jnp.concatenate(states, axis=1), nt_dims,
                                    preferred_element_type=F32)
        acc_ref[rows, :] = o_t
        return st

    def both(i, sts):
        gens = [tile_step(i, sts[0], ff_ref, False, oacc_ref),
                tile_step(jnp.where(i == 0, 0, tpb - i), sts[1], fb_ref, True, obwd_ref)]
        out = [None, None]
        while any(o is None for o in out):
            for n, gen in enumerate(gens):
                if out[n] is None:
                    try:
                        next(gen)
                    except StopIteration as done:
                        out[n] = done.value
        return tuple(out)

    s0 = jnp.zeros((HG_DK, HG_DK), F32)
    lax.fori_loop(0, tpb, both, (s0, s0))

    def readout(t, carry):
        rows = tile_rows(t)
        gate = g_ref[rows, :]
        y = _rms(oacc_ref[rows, :] + obwd_ref[rows, :], gn) * (gate * _sigmoid(gate))
        o_ref[rows, :] = y.astype(o_ref.dtype)
        return carry

    lax.fori_loop(0, tpb, readout, 0)


def _hgrn(hg, lb, gn, B, tpb):
    N = hg.shape[0]
    T = tpb * ROW_TILE
    part = lambda p: pl.BlockSpec((T, HG_DK), lambda b, h: (b, p * HG_HEADS + h))
    return pl.pallas_call(
        functools.partial(_hgrn_kernel, tpb=tpb),
        grid=(B, HG_HEADS),
        in_specs=[part(0), part(1), part(2), part(3), part(4),
                  pl.BlockSpec((None, 1, HG_DK), lambda b, h: (h, 0, 0)),
                  pl.BlockSpec((1, HG_DK), lambda b, h: (0, 0))],
        out_specs=pl.BlockSpec((T, HG_DK), lambda b, h: (b, h)),
        out_shape=jax.ShapeDtypeStruct((N, HG_K), BF16),
        scratch_shapes=[pltpu.VMEM((T, HG_DK), F32), pltpu.VMEM((T, HG_DK), F32)],
        compiler_params=_params("arbitrary", "arbitrary"),
        name="hgrn2",
    )(hg, hg, hg, hg, hg, lb, gn)


def _post_kernel(*refs, n_in):
    xs = refs[:n_in]
    ws = refs[n_in:2 * n_in]
    (bias_ref, h_ref, mod_ref, lg_ref, lb_ref, rw_ref, rb_ref,
     h1_out, bin_out, ti_out, tw_out, cnt_out) = refs[2 * n_in:]
    i = pl.program_id(0)
    y = bias_ref[...]
    for x_ref, w_ref in zip(xs, ws):
        y = y + jnp.dot(x_ref[...], w_ref[...], preferred_element_type=F32)
    m = mod_ref[...]
    h1 = _layernorm(DEEPNORM_ALPHA * h_ref[...] + m[2:3] * y, lg_ref[...], lb_ref[...])
    h1_out[...] = h1
    b_in = (h1 * (1.0 + m[4:5]) + m[3:4]).astype(BF16)
    bin_out[...] = b_in
    lg = jnp.dot(b_in, rw_ref[...], preferred_element_type=F32) + rb_ref[...]
    lane = lax.broadcasted_iota(jnp.int32, lg.shape, 1)
    vals, idxs = [], []
    member = jnp.zeros(lg.shape, F32)
    for _ in range(TOP_K):
        mx = jnp.max(lg, axis=-1, keepdims=True)
        ix = jnp.min(jnp.where(lg == mx, lane, LANES), axis=-1, keepdims=True)
        hit = lane == ix
        member = member + hit.astype(F32)
        lg = jnp.where(hit, -jnp.inf, lg)
        vals.append(mx)
        idxs.append(ix)
    es = [jnp.exp(v - vals[0]) for v in vals]
    den = es[0] + es[1] + es[2] + es[3]
    ti = jnp.zeros(lg.shape, jnp.int32)
    tw = jnp.zeros(lg.shape, F32)
    for k in range(TOP_K):
        ti = jnp.where(lane == k, idxs[k], ti)
        tw = jnp.where(lane == k, es[k] / den, tw)
    ti_out[...] = ti
    tw_out[...] = tw

    @pl.when(i == 0)
    def _():
        cnt_out[...] = jnp.zeros(cnt_out.shape, F32)

    cnt_out[0:1, :] += jnp.sum(member, axis=0, keepdims=True)


def _post(xs, ws, bias, h3, h_map, mods, mod_map, ln_g, ln_b, rw, rb, n_tok):
    D = D_MODEL
    n_in = len(xs)
    const = lambda i: (0, 0)
    row = lambda i: (i, 0)
    in_specs = ([pl.BlockSpec((ROW_TILE, x.shape[1]), row) for x in xs]
                + [pl.BlockSpec(w.shape, const) for w in ws]
                + [pl.BlockSpec((1, D), const),
                   pl.BlockSpec((None, ROW_TILE, D), h_map),
                   pl.BlockSpec((None, 6, D), mod_map),
                   pl.BlockSpec((1, D), const),
                   pl.BlockSpec((1, D), const),
                   pl.BlockSpec((D, LANES), const),
                   pl.BlockSpec((1, LANES), const)])
    return pl.pallas_call(
        functools.partial(_post_kernel, n_in=n_in),
        grid=(n_tok // ROW_TILE,),
        in_specs=in_specs,
        out_specs=[pl.BlockSpec((ROW_TILE, D), row),
                   pl.BlockSpec((ROW_TILE, D), row),
                   pl.BlockSpec((ROW_TILE, LANES), row),
                   pl.BlockSpec((ROW_TILE, LANES), row),
                   pl.BlockSpec((8, LANES), const)],
        out_shape=[jax.ShapeDtypeStruct((n_tok, D), F32),
                   jax.ShapeDtypeStruct((n_tok, D), BF16),
                   jax.ShapeDtypeStruct((n_tok, LANES), jnp.int32),
                   jax.ShapeDtypeStruct((n_tok, LANES), F32),
                   jax.ShapeDtypeStruct((8, LANES), F32)],
        compiler_params=_params("arbitrary"),
        name="post_mixer",
    )(*xs, *ws, bias, h3, mods, ln_g, ln_b, rw, rb)


def _plan_kernel(ti_ref, ps_ref, lp_out, cnt_out, goff_out, carry_ref):
    i = pl.program_id(0)

    @pl.when(i == 0)
    def _():
        carry_ref[...] = jnp.zeros(carry_ref.shape, F32)

    ti = ti_ref[...]
    lane = lax.broadcasted_iota(jnp.int32, ti.shape, 1)
    hits = [lane == ti[:, k:k + 1] for k in range(TOP_K)]
    member = hits[0].astype(F32)
    for k in range(1, TOP_K):
        member = member + hits[k].astype(F32)
    row = lax.broadcasted_iota(jnp.int32, (ROW_TILE, ROW_TILE), 0)
    col = lax.broadcasted_iota(jnp.int32, (ROW_TILE, ROW_TILE), 1)
    before = (col < row).astype(BF16)
    rank = jnp.dot(before, member.astype(BF16), preferred_element_type=F32)
    cnt = jnp.sum(member, axis=0, keepdims=True)
    er = lax.broadcasted_iota(jnp.int32, (LANES, LANES), 0)
    ec = lax.broadcasted_iota(jnp.int32, (LANES, LANES), 1)
    cnt8 = jnp.broadcast_to(cnt, (8, LANES))
    loff = jnp.dot(cnt8.astype(BF16), (er < ec).astype(BF16), preferred_element_type=F32)[0:1, :]
    base = rank + loff
    lp = jnp.zeros(ti.shape, F32)
    for k in range(TOP_K):
        pk = jnp.sum(jnp.where(hits[k], base, 0.0), axis=-1, keepdims=True)
        lp = jnp.where(lane == k, pk, lp)
    lp_out[...] = lp.astype(jnp.int32)
    cnt_out[...] = cnt8.astype(jnp.int32)
    goff_out[...] = jnp.broadcast_to(carry_ref[0:1, :] + ps_ref[...], (8, LANES)).astype(jnp.int32)
    carry_ref[0:1, :] += cnt


def _plan(ti, pstart):
    n_tok = ti.shape[0]
    nt = n_tok // ROW_TILE
    return pl.pallas_call(
        _plan_kernel,
        grid=(nt,),
        in_specs=[pl.BlockSpec((ROW_TILE, LANES), lambda i: (i, 0)),
                  pl.BlockSpec((1, LANES), lambda i: (0, 0))],
        out_specs=[pl.BlockSpec((ROW_TILE, LANES), lambda i: (i, 0)),
                   pl.BlockSpec((None, 8, LANES), lambda i: (i, 0, 0)),
                   pl.BlockSpec((None, 8, LANES), lambda i: (i, 0, 0))],
        out_shape=[jax.ShapeDtypeStruct((n_tok, LANES), jnp.int32),
                   jax.ShapeDtypeStruct((nt, 8, LANES), jnp.int32),
                   jax.ShapeDtypeStruct((nt, 8, LANES), jnp.int32)],
        scratch_shapes=[pltpu.VMEM((8, LANES), F32)],
        compiler_params=_params("arbitrary"),
        name="moe_plan",
    )(ti, pstart)


SLOT_ROWS = ROW_TILE * TOP_K
RUN_BITS = ROW_TILE.bit_length()
ROW_SHAPE = (8, LANES)


def _row_copy(src, dst, sem):
    return pltpu.make_async_copy(src, dst, sem)


def _for_each_run_piece(cnt_ref, goff_ref, tile, fn):
    def body(e, lo):
        c = cnt_ref[tile * N_EXPERTS + e]
        g = goff_ref[tile * N_EXPERTS + e]
        for b in range(RUN_BITS):
            size = 1 << b

            @pl.when(jnp.bitwise_and(lax.shift_right_logical(c, b), 1) == 1)
            def _():
                off = jnp.bitwise_and(c, size - 1)
                fn(lo + off, g + off, size)
        return lo + c

    lax.fori_loop(0, N_EXPERTS, body, 0)


def _slot_onehot(lp, n_cols):
    pos = lax.broadcasted_iota(jnp.int32, (lp.shape[0], n_cols), 1)
    hit = pos == lp[:, 0:1]
    for k in range(1, TOP_K):
        hit = jnp.logical_or(hit, pos == lp[:, k:k + 1])
    return hit


def _dispatch_kernel(bv_ref, cnt_ref, goff_ref, lp_ref, x_ref, xs_hbm, zero_ref, buf_ref, sem, zsem):
    n_blocks = xs_hbm.shape[0] // MOE_TILE
    i = pl.program_id(0)
    nt = pl.num_programs(0)
    slot = i % 2

    def wait_slot(s):
        _row_copy(buf_ref.at[s], xs_hbm.at[pl.ds(0, SLOT_ROWS)], sem.at[s]).wait()

    @pl.when(pl.program_id(0) == 0)
    def _():
        zero_ref[...] = jnp.zeros(zero_ref.shape, F32)

        def fill(wait):
            def body(j, carry):
                @pl.when(bv_ref[j] < MOE_TILE)
                def _():
                    rows = pl.ds(pl.multiple_of(j * MOE_TILE, MOE_TILE), MOE_TILE)
                    cp = _row_copy(zero_ref, xs_hbm.at[rows], zsem)
                    if wait:
                        cp.wait()
                    else:
                        cp.start()
                return carry
            return body

        lax.fori_loop(0, n_blocks, fill(False), 0)
        lax.fori_loop(0, n_blocks, fill(True), 0)

    @pl.when(i >= 2)
    def _():
        wait_slot(slot)

    onehot = _slot_onehot(lp_ref[...], SLOT_ROWS).astype(BF16)
    rows = lax.dot_general(onehot, x_ref[...], (((0,), (0,)), ((), ())), preferred_element_type=F32)
    buf_ref[slot] = rows.reshape((SLOT_ROWS,) + ROW_SHAPE)

    def copy_out(lo, g, size):
        _row_copy(buf_ref.at[slot, pl.ds(lo, size)], xs_hbm.at[pl.ds(g, size)], sem.at[slot]).start()

    _for_each_run_piece(cnt_ref, goff_ref, i, copy_out)

    @pl.when(i == nt - 1)
    def _():
        wait_slot(slot)

        @pl.when(nt >= 2)
        def _():
            wait_slot(1 - slot)


def _dispatch(block_valid, cnt_flat, goff_flat, lp, b_in, n_rows):
    n_tok, D = b_in.shape
    grid_spec = pltpu.PrefetchScalarGridSpec(
        num_scalar_prefetch=3,
        grid=(n_tok // ROW_TILE,),
        in_specs=[pl.BlockSpec((ROW_TILE, LANES), lambda i, *_: (i, 0)),
                  pl.BlockSpec((ROW_TILE, D), lambda i, *_: (i, 0))],
        out_specs=pl.BlockSpec(memory_space=pl.ANY),
        scratch_shapes=[pltpu.VMEM((MOE_TILE,) + ROW_SHAPE, F32), pltpu.VMEM((2, SLOT_ROWS) + ROW_SHAPE, F32),
                        pltpu.SemaphoreType.DMA((2,)), pltpu.SemaphoreType.DMA(())],
    )
    return pl.pallas_call(
        _dispatch_kernel,
        grid_spec=grid_spec,
        out_shape=jax.ShapeDtypeStruct((n_rows,) + ROW_SHAPE, F32),
        compiler_params=_params("arbitrary"),
        name="moe_dispatch",
    )(block_valid, cnt_flat, goff_flat, lp, b_in)


GU_GROUP = 2 * LANES


def _pair_split_matrix():
    r = np.arange(GU_GROUP)
    p = np.zeros((GU_GROUP, GU_GROUP), np.float32)
    p[r, np.where(r % 2 == 0, r // 2, LANES + r // 2)] = 1.0
    return jnp.asarray(p, BF16)


def _expert_kernel(be_ref, bv_ref, x_ref, wgu_ref, bgu_ref, wd_ref, bd_ref, split_ref, y_ref, wgu_s, wd_s):
    j = pl.program_id(0)
    valid = bv_ref[j]
    n_groups = wgu_s.shape[1] // GU_GROUP
    changed = jnp.logical_or(j == 0, be_ref[j] != be_ref[jnp.maximum(j - 1, 0)])

    @pl.when(changed)
    def _():
        split = split_ref[...]
        for g in range(n_groups):
            cols = slice(g * GU_GROUP, (g + 1) * GU_GROUP)
            wgu_s[:, cols] = jnp.dot(wgu_ref[:, cols].astype(BF16), split,
                                     preferred_element_type=F32).astype(BF16)
        wd_s[...] = wd_ref[...].astype(BF16)

    @pl.when(valid > 0)
    def _():
        def gate_up(r0):
            x = x_ref[r0:r0 + EXPERT_SUB].reshape(EXPERT_SUB, wd_s.shape[1])
            rowi = lax.broadcasted_iota(jnp.int32, x.shape, 0) + r0
            x = jnp.where(rowi < valid, x, 0.0).astype(BF16)
            return jnp.dot(x, wgu_s[...], preferred_element_type=F32) + bgu_ref[...]

        def act_down(r0, gu):
            acts = []
            for g in range(n_groups):
                gate = jnp.minimum(gu[:, g * GU_GROUP:g * GU_GROUP + LANES], SWIGLU_LIMIT)
                up = jnp.clip(gu[:, g * GU_GROUP + LANES:(g + 1) * GU_GROUP], -SWIGLU_LIMIT, SWIGLU_LIMIT)
                acts.append(((up + 1.0) * gate * _sigmoid(SWIGLU_ALPHA * gate)).astype(BF16))
            act = jnp.concatenate(acts, axis=1)
            y = jnp.dot(act, wd_s[...], preferred_element_type=F32) + bd_ref[...]
            y_ref[r0:r0 + EXPERT_SUB] = y.reshape((EXPERT_SUB,) + ROW_SHAPE)

        starts = list(range(0, MOE_TILE, EXPERT_SUB))
        gu = gate_up(starts[0])
        for n, r0 in enumerate(starts):
            nxt = gate_up(starts[n + 1]) if n + 1 < len(starts) else None
            act_down(r0, gu)
            gu = nxt

    @pl.when(valid <= 0)
    def _():
        y_ref[...] = jnp.zeros(y_ref.shape, F32)


def _experts(block_e, block_valid, xs, layer, w_gu, b_gu_split, w_down, b_down):
    P = xs.shape[0]
    _, E, D, W = w_gu.shape
    DE = w_down.shape[2]
    wmap = lambda j, be, bv: (layer, be[j], 0, 0)
    grid_spec = pltpu.PrefetchScalarGridSpec(
        num_scalar_prefetch=2,
        grid=(P // MOE_TILE,),
        in_specs=[pl.BlockSpec((MOE_TILE,) + ROW_SHAPE, lambda j, be, bv: (j, 0, 0)),
                  pl.BlockSpec((None, None, D, W), wmap),
                  pl.BlockSpec((None, None, 1, W), wmap),
                  pl.BlockSpec((None, None, DE, D), wmap),
                  pl.BlockSpec((None, None, 1, D), wmap),
                  pl.BlockSpec((GU_GROUP, GU_GROUP), lambda j, be, bv: (0, 0))],
        out_specs=pl.BlockSpec((MOE_TILE,) + ROW_SHAPE, lambda j, be, bv: (j, 0, 0)),
        scratch_shapes=[pltpu.VMEM((D, W), BF16), pltpu.VMEM((DE, D), BF16)],
    )
    return pl.pallas_call(
        _expert_kernel,
        grid_spec=grid_spec,
        out_shape=jax.ShapeDtypeStruct((P,) + ROW_SHAPE, F32),
        compiler_params=_params("arbitrary"),
        name="moe_experts",
    )(block_e, block_valid, xs, w_gu, b_gu_split, w_down, b_down, _pair_split_matrix())


def _combine_kernel(cnt_ref, goff_ref, ys_hbm, lp_ref, tw_ref, h_ref, mod_ref, lg_ref, lb_ref, o_ref,
                    buf_ref, sem):
    i = pl.program_id(0)
    nt = pl.num_programs(0)
    slot = i % 2

    def fetch(tile, s):
        def copy_in(lo, g, size):
            _row_copy(ys_hbm.at[pl.ds(g, size)], buf_ref.at[s, pl.ds(lo, size)], sem.at[s]).start()

        _for_each_run_piece(cnt_ref, goff_ref, tile, copy_in)

    @pl.when(i == 0)
    def _():
        fetch(0, 0)

    @pl.when(i + 1 < nt)
    def _():
        fetch(i + 1, 1 - slot)

    _row_copy(ys_hbm.at[pl.ds(0, SLOT_ROWS)], buf_ref.at[slot], sem.at[slot]).wait()

    lp = lp_ref[...]
    tw = tw_ref[...]
    pos = lax.broadcasted_iota(jnp.int32, (ROW_TILE, SLOT_ROWS), 1)
    wsel = jnp.where(pos == lp[:, 0:1], tw[:, 0:1], 0.0)
    for k in range(1, TOP_K):
        wsel = wsel + jnp.where(pos == lp[:, k:k + 1], tw[:, k:k + 1], 0.0)
    w_hi = wsel.astype(BF16)
    w_lo = (wsel - w_hi.astype(F32)).astype(BF16)
    ys = buf_ref[slot].reshape(SLOT_ROWS, h_ref.shape[1]).astype(BF16)
    f = (jnp.dot(w_hi, ys, preferred_element_type=F32) + jnp.dot(w_lo, ys, preferred_element_type=F32))
    m = mod_ref[...]
    o_ref[...] = _layernorm(DEEPNORM_ALPHA * h_ref[...] + m[5:6] * f, lg_ref[...], lb_ref[...])


def _combine(cnt_flat, goff_flat, ys, lp, tw, h1, mods, mod_map, ln_g, ln_b):
    n_tok, D = h1.shape
    const = lambda i, *_: (0, 0)
    row = lambda i, *_: (i, 0)
    grid_spec = pltpu.PrefetchScalarGridSpec(
        num_scalar_prefetch=2,
        grid=(n_tok // ROW_TILE,),
        in_specs=[pl.BlockSpec(memory_space=pl.ANY),
                  pl.BlockSpec((ROW_TILE, LANES), row),
                  pl.BlockSpec((ROW_TILE, LANES), row),
                  pl.BlockSpec((ROW_TILE, D), row),
                  pl.BlockSpec((None, 6, D), lambda i, *_: mod_map(i)),
                  pl.BlockSpec((1, D), const),
                  pl.BlockSpec((1, D), const)],
        out_specs=pl.BlockSpec((ROW_TILE, D), row),
        scratch_shapes=[pltpu.VMEM((2, SLOT_ROWS) + ROW_SHAPE, F32), pltpu.SemaphoreType.DMA((2,))],
    )
    return pl.pallas_call(
        _combine_kernel,
        grid_spec=grid_spec,
        out_shape=jax.ShapeDtypeStruct((n_tok, D), F32),
        compiler_params=_params("arbitrary"),
        name="moe_combine",
    )(cnt_flat, goff_flat, ys, lp, tw, h1, mods, ln_g, ln_b)


def _moe(b_in, ti, tw, counts, h1, mods, mod_map, ln_g, ln_b, layer, w_gu, b_gu, w_down, b_down):
    n_tok = b_in.shape[0]
    cnt = counts[0, :N_EXPERTS].astype(jnp.int32)
    pcnt = (cnt + MOE_TILE - 1) // MOE_TILE * MOE_TILE
    pends = jnp.cumsum(pcnt)
    pstart = pends - pcnt
    n_blocks = -(-(n_tok * TOP_K + N_EXPERTS * (MOE_TILE - 1)) // MOE_TILE)
    blk0 = jnp.arange(n_blocks, dtype=jnp.int32) * MOE_TILE
    block_e = jnp.minimum(jnp.sum((pends[None, :] <= blk0[:, None]).astype(jnp.int32), axis=1), N_EXPERTS - 1)
    block_valid = jnp.clip(cnt[block_e] - (blk0 - pstart[block_e]), 0, MOE_TILE).astype(jnp.int32)
    ps_row = jnp.zeros((1, LANES), F32).at[0, :N_EXPERTS].set(pstart.astype(F32))
    lp, cnt_t, goff_t = _plan(ti, ps_row)
    cnt_flat = cnt_t[:, 0, :N_EXPERTS].reshape(-1)
    goff_flat = goff_t[:, 0, :N_EXPERTS].reshape(-1)
    xs = _dispatch(block_valid, cnt_flat, goff_flat, lp, b_in, n_blocks * MOE_TILE)
    L, E, W = b_gu.shape
    b_split = b_gu.reshape(L, E, W // GU_GROUP, LANES, 2).transpose(0, 1, 2, 4, 3).reshape(L, E, 1, W)
    ys = _experts(block_e, block_valid, xs, layer, w_gu, b_split, w_down, b_down.reshape(L, E, 1, -1))
    return _combine(cnt_flat, goff_flat, ys, lp, tw, h1, mods, mod_map, ln_g, ln_b)


def _router_weights(router_w, router_b):
    D = router_w.shape[0]
    rw = jnp.zeros((D, LANES), F32).at[:, :N_EXPERTS].set(router_w).astype(BF16)
    rb = jnp.full((1, LANES), -jnp.inf, F32).at[0, :N_EXPERTS].set(router_b)
    return rw, rb


def _glu_kernel(h_ref, mod_ref, w_ref, b_ref, u_out):
    m = mod_ref[...]
    a = (h_ref[...] * (1.0 + m[1:2]) + m[0:1]).astype(BF16)
    z = jnp.dot(a, w_ref[...], preferred_element_type=F32) + b_ref[...]
    half = z.shape[1] // 2
    u_out[...] = z[:, :half] * _sigmoid(z[:, half:])


def _glu(h3, mods, w, b, B, tpb_lat):
    D = D_MODEL
    return pl.pallas_call(
        _glu_kernel,
        grid=(B, tpb_lat),
        in_specs=[pl.BlockSpec((None, ROW_TILE, D), lambda b, j: (b, j + 1, 0)),
                  pl.BlockSpec((None, 6, D), lambda b, j: (b, 0, 0)),
                  pl.BlockSpec(w.shape, lambda b, j: (0, 0)),
                  pl.BlockSpec((1, w.shape[1]), lambda b, j: (0, 0))],
        out_specs=pl.BlockSpec((None, ROW_TILE, D), lambda b, j: (b, j, 0)),
        out_shape=jax.ShapeDtypeStruct((B, tpb_lat * ROW_TILE, D), F32),
        compiler_params=_params("arbitrary", "arbitrary"),
        name="conv_glu",
    )(h3, mods, w, b)


def _dwconv_kernel(prev_ref, cur_ref, next_ref, w_ref, b_ref, lg_ref, lb_ref, o_ref, win_ref, *, tpb_lat):
    j = pl.program_id(1)
    H = CONV_HALO
    zero = jnp.zeros((H, cur_ref.shape[1]), F32)
    win_ref[0:H, :] = jnp.where(j > 0, prev_ref[...], zero)
    win_ref[H:H + ROW_TILE, :] = cur_ref[...]
    win_ref[H + ROW_TILE:, :] = jnp.where(j < tpb_lat - 1, next_ref[...], zero)
    w = w_ref[...]
    off = H - CONV_WIDTH // 2
    acc = jnp.zeros(cur_ref.shape, F32) + b_ref[...]
    for k in range(CONV_WIDTH):
        acc = acc + win_ref[off + k:off + k + ROW_TILE, :] * w[k:k + 1, :]
    y = _layernorm(acc, lg_ref[...], lb_ref[...])
    o_ref[...] = (y * _sigmoid(y)).astype(o_ref.dtype)


def _dwconv(u, w, b, ln_g, ln_b, B, tpb_lat):
    D = D_MODEL
    n = tpb_lat * ROW_TILE
    per = ROW_TILE // CONV_HALO
    last = n // CONV_HALO - 1
    const = lambda b_, j: (0, 0)
    return pl.pallas_call(
        functools.partial(_dwconv_kernel, tpb_lat=tpb_lat),
        grid=(B, tpb_lat),
        in_specs=[pl.BlockSpec((None, CONV_HALO, D), lambda b_, j: (b_, jnp.maximum(j * per - 1, 0), 0)),
                  pl.BlockSpec((None, ROW_TILE, D), lambda b_, j: (b_, j, 0)),
                  pl.BlockSpec((None, CONV_HALO, D), lambda b_, j: (b_, jnp.minimum((j + 1) * per, last), 0)),
                  pl.BlockSpec(w.shape, const),
                  pl.BlockSpec((1, D), const),
                  pl.BlockSpec((1, D), const),
                  pl.BlockSpec((1, D), const)],
        out_specs=pl.BlockSpec((ROW_TILE, D), lambda b_, j: (b_ * tpb_lat + j, 0)),
        out_shape=jax.ShapeDtypeStruct((B * n, D), BF16),
        scratch_shapes=[pltpu.VMEM((ROW_TILE + 2 * CONV_HALO, D), F32)],
        compiler_params=_params("arbitrary", "arbitrary"),
        name="conv_dw",
    )(u, u, u, w, b, ln_g, ln_b)


def kernel(x, c, ctx, c_ctx, mod_w, mod_b, ln1_g, ln1_b, ln2_g, ln2_b, mix_w_in, mla_q_norm, mla_kv_norm, mla_w_uq, mla_w_uk, mla_w_uv, hg_lb_logits, hg_norm, mix_w_out, conv_pw1_w, conv_pw1_b, conv_dw_w, conv_dw_b, conv_ln_g, conv_ln_b, conv_pw2_w, conv_pw2_b, router_w, router_b, exp_w_gu, exp_b_gu, exp_w_down, exp_b_down):
    B, n, D = x.shape
    tc = ctx.shape[1]
    assert D == D_MODEL and tc == ROW_TILE and n % ROW_TILE == 0 and n % GRID_W == 0
    assert B + 1 <= 16
    T = tc + n
    tpb = T // ROW_TILE
    tpb_lat = n // ROW_TILE
    N = B * T
    row2 = lambda v: v.reshape(1, -1)

    cond = jnp.zeros((16, D), F32).at[:B].set(c).at[B].set(c_ctx)
    mods = _modulation(cond, mod_w, mod_b)[:, :B + 1].reshape(DEPTH, B + 1, 6, D)
    lb_all = jnp.cumsum(jax.nn.softmax(hg_lb_logits.astype(F32), axis=0), axis=0)

    h0 = jnp.concatenate([ctx, x], axis=1)
    win, wq2, wkv = _inproj_weights(mix_w_in[0], mla_w_uq[0], mla_w_uk[0], mla_w_uv[0])
    cos_t, sin_t = _rope_tables(n, tc)
    q, k, v, hg = _inproj(h0.reshape(N, D), mods[0], win, row2(mla_q_norm[0]), row2(mla_kv_norm[0]),
                          wq2, wkv, cos_t, sin_t, B, tpb)
    att = _attention(q, k, v, B, tpb, tc)
    hgo = _hgrn(hg, lb_all[0].reshape(HG_HEADS, 1, HG_DK), row2(hg_norm[0]), B, tpb)
    w_out = mix_w_out[0].astype(BF16)
    n_att = MLA_HEADS * MLA_V
    rw, rb = _router_weights(router_w[0], router_b[0])
    mod_map0 = lambda i: (jnp.where(i % tpb == 0, B, i // tpb), 0, 0)
    h1, b_in, ti, tw, counts = _post(
        [att, hgo], [w_out[:n_att], w_out[n_att:]], jnp.zeros((1, D), F32),
        h0.reshape(N // ROW_TILE, ROW_TILE, D), lambda i: (i, 0, 0), mods[0], mod_map0,
        row2(ln1_g[0]), row2(ln1_b[0]), rw, rb, N)
    ew = (exp_w_gu, exp_b_gu, exp_w_down, exp_b_down)
    h2 = _moe(b_in, ti, tw, counts, h1, mods[0], mod_map0, row2(ln2_g[0]), row2(ln2_b[0]), 0, *ew)

    n_lat = B * n
    h2_3 = h2.reshape(B * tpb, ROW_TILE, D)
    u = _glu(h2.reshape(B, T, D), mods[1], conv_pw1_w[0].astype(BF16), row2(conv_pw1_b[0]), B, tpb_lat)
    cv = _dwconv(u, conv_dw_w[0], row2(conv_dw_b[0]), row2(conv_ln_g[0]), row2(conv_ln_b[0]), B, tpb_lat)
    rw, rb = _router_weights(router_w[1], router_b[1])
    mod_map1 = lambda i: (i // tpb_lat, 0, 0)
    h_map1 = lambda i: ((i // tpb_lat) * tpb + i % tpb_lat + 1, 0, 0)
    h3, b_in, ti, tw, counts = _post(
        [cv], [conv_pw2_w[0].astype(BF16)], row2(conv_pw2_b[0]),
        h2_3, h_map1, mods[1], mod_map1, row2(ln1_g[1]), row2(ln1_b[1]), rw, rb, n_lat)
    h4 = _moe(b_in, ti, tw, counts, h3, mods[1], mod_map1, row2(ln2_g[1]), row2(ln2_b[1]), 1, *ew)
    return h4.reshape(B, n, D)
```

```python
import functools

import numpy as np
import jax
import jax.numpy as jnp
from jax import lax
from jax.experimental import pallas as pl
from jax.experimental.pallas import tpu as pltpu

F32 = jnp.float32
BF16 = jnp.bfloat16

D_MODEL = 1024
DEPTH = 2
GRID_W = 64
MLA_HEADS = 8
MLA_Q_LORA = 384
MLA_KV_LORA = 256
MLA_NOPE = 64
MLA_ROPE = 32
MLA_V = 64
MLA_SCALE = (MLA_NOPE + MLA_ROPE) ** -0.5
LOG2_E = 1.4426950408889634
ROPE_BASE = 10000.0
HG_HEADS = 4
HG_DK = 128
HG_CHUNK = 32
HG_K = HG_HEADS * HG_DK
CONV_WIDTH = 31
N_EXPERTS = 32
TOP_K = 4
SWIGLU_LIMIT = 7.0
SWIGLU_ALPHA = 1.702
LN_EPS = 1e-5
RMS_EPS = 1e-6
DEEPNORM_ALPHA = (2.0 * DEPTH) ** 0.25

LANES = 128
ROW_TILE = 256
HEAD_BLOCK = 128
KEY_CHUNK = 1024
MOE_TILE = 512
CONV_HALO = 16
IN_WIDTH = 3456
HG_PARTS = 5 * HG_K
VMEM_LIMIT = 56 * 1024 * 1024


def _params(*sem):
    return pltpu.CompilerParams(dimension_semantics=sem, vmem_limit_bytes=VMEM_LIMIT)


def _sigmoid(x):
    return 1.0 / (1.0 + jnp.exp(-x))


def _layernorm(x, g, b):
    xc = x - jnp.mean(x, axis=-1, keepdims=True)
    var = jnp.mean(xc * xc, axis=-1, keepdims=True)
    return xc * lax.rsqrt(var + LN_EPS) * g + b


def _rms(x, g):
    return x * lax.rsqrt(jnp.mean(x * x, axis=-1, keepdims=True) + RMS_EPS) * g


def _mod_kernel(c_ref, w_ref, b_ref, o_ref):
    c = c_ref[...]
    s = (c * _sigmoid(c)).astype(BF16)
    o_ref[...] = jnp.dot(s, w_ref[...].astype(BF16), preferred_element_type=F32) + b_ref[...]


def _modulation(cond, mod_w, mod_b):
    L, D, W = mod_w.shape
    R = cond.shape[0]
    return pl.pallas_call(
        _mod_kernel,
        grid=(L, W // D),
        in_specs=[pl.BlockSpec((R, D), lambda l, n: (0, 0)),
                  pl.BlockSpec((None, D, D), lambda l, n: (l, 0, n)),
                  pl.BlockSpec((None, 1, D), lambda l, n: (l, 0, n))],
        out_specs=pl.BlockSpec((None, R, D), lambda l, n: (l, 0, n)),
        out_shape=jax.ShapeDtypeStruct((L, R, W), F32),
        compiler_params=_params("arbitrary", "arbitrary"),
        name="adaln_mod",
    )(cond, mod_w, mod_b.reshape(L, 1, W))


def _inproj_kernel(ctx_ref, x_ref, mod_ref, win_ref, qn_ref, kvn_ref, wq_ref, wkv_ref, c_ref, s_ref,
                   q_out, k_out, v_out, hg_out, *, tpb):
    m = mod_ref[...]
    h = jnp.where(pl.program_id(0) % tpb == 0, ctx_ref[...], x_ref[...])
    a = (h * (1.0 + m[1:2]) + m[0:1]).astype(BF16)
    z = jnp.dot(a, win_ref[...], preferred_element_type=F32)
    cos = c_ref[...]
    sin = s_ref[...]
    cos_h = jnp.concatenate([cos] * MLA_HEADS, axis=1)
    sin_h = jnp.concatenate([sin] * MLA_HEADS, axis=1)
    width = MLA_HEADS * HEAD_BLOCK

    cqn = _rms(z[:, :MLA_Q_LORA], qn_ref[...]).astype(BF16)
    qq = jnp.dot(cqn, wq_ref[...], preferred_element_type=F32)
    q_out[...] = ((qq[:, :width] * cos_h + qq[:, width:] * sin_h) * (MLA_SCALE * LOG2_E)).astype(BF16)

    c0 = MLA_Q_LORA + MLA_KV_LORA
    ckvn = _rms(z[:, MLA_Q_LORA:c0], kvn_ref[...]).astype(BF16)
    kv = jnp.dot(ckvn, wkv_ref[...], preferred_element_type=F32)
    r0 = c0 + HG_PARTS
    kr = z[:, r0:r0 + HEAD_BLOCK] * cos + z[:, r0 + HEAD_BLOCK:r0 + 2 * HEAD_BLOCK] * sin
    k_out[...] = (kv[:, :width] + jnp.concatenate([kr] * MLA_HEADS, axis=1)).astype(BF16)
    v_out[...] = kv[:, width:].astype(BF16)
    hg_out[...] = z[:, c0:r0]


def _rope_partner(w):
    shp = w.shape
    wr = w.reshape(shp[:-1] + (shp[-1] // 2, 2))
    return jnp.stack([-wr[..., 1], wr[..., 0]], axis=-1).reshape(shp)


def _inproj_weights(w_in, w_uq, w_uk, w_uv):
    D = w_in.shape[0]
    o_kr = MLA_Q_LORA + MLA_KV_LORA
    w_kr = w_in[:, o_kr:o_kr + MLA_ROPE]
    z_nope = jnp.zeros((D, MLA_NOPE), F32)
    z_pad = jnp.zeros((D, HEAD_BLOCK - MLA_NOPE - MLA_ROPE), F32)
    krb = jnp.concatenate([z_nope, w_kr, z_pad], axis=1)
    krb_p = jnp.concatenate([z_nope, _rope_partner(w_kr), z_pad], axis=1)
    win = jnp.concatenate([w_in[:, :o_kr], w_in[:, o_kr + MLA_ROPE:], krb, krb_p], axis=1).astype(BF16)

    R = w_uq.shape[0]
    wq = w_uq.reshape(R, MLA_HEADS, MLA_NOPE + MLA_ROPE)
    nope, rope = wq[..., :MLA_NOPE], wq[..., MLA_NOPE:]
    zp = jnp.zeros((R, MLA_HEADS, HEAD_BLOCK - MLA_NOPE - MLA_ROPE), F32)
    zn = jnp.zeros((R, MLA_HEADS, MLA_NOPE), F32)
    q_main = jnp.concatenate([nope, rope, zp], axis=-1).reshape(R, MLA_HEADS * HEAD_BLOCK)
    q_part = jnp.concatenate([zn, _rope_partner(rope), zp], axis=-1).reshape(R, MLA_HEADS * HEAD_BLOCK)
    wq2 = jnp.concatenate([q_main, q_part], axis=1).astype(BF16)

    C = w_uk.shape[0]
    wk = w_uk.reshape(C, MLA_HEADS, MLA_NOPE)
    wk = jnp.concatenate([wk, jnp.zeros((C, MLA_HEADS, HEAD_BLOCK - MLA_NOPE), F32)], axis=-1)
    wv = w_uv.reshape(C, MLA_HEADS // 2, 2, MLA_V)
    zv = jnp.zeros((C, MLA_HEADS // 2, MLA_V), F32)
    wv = jnp.stack([jnp.concatenate([wv[:, :, 0], zv], -1), jnp.concatenate([zv, wv[:, :, 1]], -1)], axis=2)
    wkv = jnp.concatenate([wk.reshape(C, -1), wv.reshape(C, -1)], axis=1).astype(BF16)
    return win, wq2, wkv


def _rope_tables(n, tc):
    rows = n // GRID_W
    pos_r = jnp.repeat(jnp.arange(rows, dtype=F32), GRID_W)
    pos_c = jnp.tile(jnp.arange(GRID_W, dtype=F32), rows)
    half = MLA_ROPE // 2
    inv = jnp.power(ROPE_BASE, -jnp.arange(0, half, 2, dtype=F32) / half)
    ang = jnp.concatenate([pos_r[:, None] * inv, pos_c[:, None] * inv], axis=-1)
    cos2 = jnp.repeat(jnp.cos(ang), 2, axis=-1)
    sin2 = jnp.repeat(jnp.sin(ang), 2, axis=-1)
    pad = HEAD_BLOCK - MLA_NOPE - MLA_ROPE
    cos_l = jnp.concatenate([jnp.ones((n, MLA_NOPE), F32), cos2, jnp.ones((n, pad), F32)], axis=1)
    sin_l = jnp.concatenate([jnp.zeros((n, MLA_NOPE), F32), sin2, jnp.zeros((n, pad), F32)], axis=1)
    cos_t = jnp.concatenate([jnp.ones((tc, HEAD_BLOCK), F32), cos_l], axis=0)
    sin_t = jnp.concatenate([jnp.zeros((tc, HEAD_BLOCK), F32), sin_l], axis=0)
    return cos_t, sin_t


def _ctx_map(tpb):
    return lambda i: (i // tpb, 0, 0)


def _lat_map(tpb):
    return lambda i: (i // tpb, jnp.maximum(i % tpb - 1, 0), 0)


def _inproj(ctx, x, mods, win, qn, kvn, wq2, wkv, cos_t, sin_t, B, tpb):
    D = x.shape[2]
    N = B * tpb * ROW_TILE
    ctx_row = mods.shape[0] - 1
    width = MLA_HEADS * HEAD_BLOCK
    const = lambda i: (0, 0)
    return pl.pallas_call(
        functools.partial(_inproj_kernel, tpb=tpb),
        grid=(N // ROW_TILE,),
        in_specs=[pl.BlockSpec((None, ROW_TILE, D), _ctx_map(tpb)),
                  pl.BlockSpec((None, ROW_TILE, D), _lat_map(tpb)),
                  pl.BlockSpec((None, 6, D), lambda i: (jnp.where(i % tpb == 0, ctx_row, i // tpb), 0, 0)),
                  pl.BlockSpec(win.shape, const),
                  pl.BlockSpec(qn.shape, const),
                  pl.BlockSpec(kvn.shape, const),
                  pl.BlockSpec(wq2.shape, const),
                  pl.BlockSpec(wkv.shape, const),
                  pl.BlockSpec((ROW_TILE, HEAD_BLOCK), lambda i: (i % tpb, 0)),
                  pl.BlockSpec((ROW_TILE, HEAD_BLOCK), lambda i: (i % tpb, 0))],
        out_specs=[pl.BlockSpec((ROW_TILE, width), lambda i: (i, 0)),
                   pl.BlockSpec((ROW_TILE, width), lambda i: (i, 0)),
                   pl.BlockSpec((ROW_TILE, width), lambda i: (i, 0)),
                   pl.BlockSpec((ROW_TILE, HG_PARTS), lambda i: (i, 0))],
        out_shape=[jax.ShapeDtypeStruct((N, width), BF16),
                   jax.ShapeDtypeStruct((N, width), BF16),
                   jax.ShapeDtypeStruct((N, width), BF16),
                   jax.ShapeDtypeStruct((N, HG_PARTS), F32)],
        compiler_params=_params("arbitrary"),
        name="inproj",
    )(ctx, x, mods, win, qn, kvn, wq2, wkv, cos_t, sin_t)


ATTN_HEADS = 4
SCORE_LOOKAHEAD = 1


def _attn_kernel(q_ref, k_ref, v_ref, o_ref, *, chunks):
    j = pl.program_id(2)

    def scores(hh, n_chunks):
        lo, hi = hh * HEAD_BLOCK, (hh + 1) * HEAD_BLOCK
        q = q_ref[:, lo:hi]
        ss, m = [], None
        for s0, s1 in chunks[:n_chunks]:
            s = lax.dot_general(q, k_ref[s0:s1, lo:hi], (((1,), (1,)), ((), ())), preferred_element_type=F32)
            mc = jnp.max(s, axis=-1, keepdims=True)
            m = mc if m is None else jnp.maximum(m, mc)
            ss.append(s)
        return ss, m

    def softmax_values(hh, scored, n_chunks):
        lo, hi = hh * HEAD_BLOCK, (hh + 1) * HEAD_BLOCK
        ss, m = scored
        l = acc = None
        for s, (s0, s1) in zip(ss, chunks[:n_chunks]):
            p = jnp.exp2(s - m)
            lc = jnp.sum(p, axis=-1, keepdims=True)
            ac = jnp.dot(p.astype(BF16), v_ref[s0:s1, lo:hi], preferred_element_type=F32)
            l = lc if l is None else l + lc
            acc = ac if acc is None else acc + ac
        return acc / l

    def pairs(n_chunks):
        heads = [None] * ATTN_HEADS
        ss = [scores(hh, n_chunks) for hh in range(min(SCORE_LOOKAHEAD, ATTN_HEADS))]
        for hh in range(ATTN_HEADS):
            if hh + SCORE_LOOKAHEAD < ATTN_HEADS:
                ss.append(scores(hh + SCORE_LOOKAHEAD, n_chunks))
            heads[hh] = softmax_values(hh, ss[hh], n_chunks)
            ss[hh] = None
        outs = [heads[hh] + heads[hh + 1] for hh in range(0, ATTN_HEADS, 2)]
        o_ref[...] = jnp.concatenate(outs, axis=1).astype(o_ref.dtype)

    @pl.when(j == 0)
    def _():
        pairs(1)

    @pl.when(j > 0)
    def _():
        pairs(len(chunks))


def _attention(q, k, v, B, tpb, tc):
    N = q.shape[0]
    T = tpb * ROW_TILE
    kc = min(KEY_CHUNK, T - tc)
    chunks = ((0, tc),) + tuple((s, s + kc) for s in range(tc, T, kc))
    width = ATTN_HEADS * HEAD_BLOCK
    return pl.pallas_call(
        functools.partial(_attn_kernel, chunks=chunks),
        grid=(B, MLA_HEADS // ATTN_HEADS, tpb),
        in_specs=[pl.BlockSpec((ROW_TILE, width), lambda b, p, j: (b * tpb + j, p)),
                  pl.BlockSpec((T, width), lambda b, p, j: (b, p)),
                  pl.BlockSpec((T, width), lambda b, p, j: (b, p))],
        out_specs=pl.BlockSpec((ROW_TILE, ATTN_HEADS * MLA_V), lambda b, p, j: (b * tpb + j, p)),
        out_shape=jax.ShapeDtypeStruct((N, MLA_HEADS * MLA_V), BF16),
        compiler_params=_params("arbitrary", "arbitrary", "arbitrary"),
        name="mla_attention",
    )(q, k, v)


def _hgrn_kernel(q_ref, ff_ref, fb_ref, v_ref, g_ref, lb_ref, gn_ref, o_ref, oacc_ref, obwd_ref, *, tpb):
    lb = lb_ref[...]
    gn = gn_ref[...]
    C = HG_CHUNK
    per_tile = ROW_TILE // C
    row = lax.broadcasted_iota(jnp.int32, (ROW_TILE, ROW_TILE), 0)
    col = lax.broadcasted_iota(jnp.int32, (ROW_TILE, ROW_TILE), 1)
    same = (row // C) == (col // C)
    ones_blk = same.astype(BF16)
    row_chunk = lax.broadcasted_iota(jnp.int32, (ROW_TILE, HG_DK), 0) // C
    nt_dims = (((1,), (1,)), ((), ()))
    tn_dims = (((0,), (0,)), ((), ()))

    def tile_rows(t):
        return pl.ds(pl.multiple_of(t * ROW_TILE, ROW_TILE), ROW_TILE)

    def chunk_sums(m01, g3):
        return sum(jnp.dot(m01, part, preferred_element_type=F32) for part in g3)

    def per_chunk_blocks(x):
        zero = jnp.zeros(x.shape, x.dtype)
        return jnp.concatenate([jnp.where(row_chunk == c, x, zero) for c in range(per_tile)], axis=1)

    def tile_step(t, st, f_ref, reverse, acc_ref):
        rows = tile_rows(t)
        f = lb + (1.0 - lb) * _sigmoid(f_ref[rows, :])
        kk = 1.0 - f
        g = jnp.log(f)
        g_hi = g.astype(BF16)
        r1 = g - g_hi.astype(F32)
        g_mid = r1.astype(BF16)
        g3 = (g_hi, g_mid, (r1 - g_mid.astype(F32)).astype(BF16))
        in_chunk = same & ((col >= row) if reverse else (col <= row))
        bcum = chunk_sums(in_chunk.astype(BF16), g3)
        btot = chunk_sums(ones_blk, g3)
        yield
        qin = (q_ref[rows, :] * jnp.exp(bcum)).astype(BF16)
        kdec = (kk * jnp.exp(-bcum)).astype(BF16)
        krem = (kk * jnp.exp(btot - bcum)).astype(BF16)
        dec = jnp.exp(btot)
        v = v_ref[rows, :].astype(BF16)
        a = lax.dot_general(qin, kdec, nt_dims, preferred_element_type=F32)
        ut = lax.dot_general(v, per_chunk_blocks(krem), tn_dims, preferred_element_type=F32)
        yield
        a = jnp.where(in_chunk, a, 0.0).astype(BF16)
        o_t = jnp.dot(a, v, preferred_element_type=F32)
        states = [None] * per_tile
        for c in (range(per_tile - 1, -1, -1) if reverse else range(per_tile)):
            states[c] = st.astype(BF16)
            st = st * dec[c * C:c * C + 1, :] + ut[:, c * HG_DK:(c + 1) * HG_DK]
        yield
        o_t = o_t + lax.dot_general(per_chunk_blocks(qin), jnp.concatenate(states, axis=1), nt_dims,
                                    preferred_element_type=F32)
        acc_ref[rows, :] = o_t
        return st

    def both(i, sts):
        gens = [tile_step(i, sts[0], ff_ref, False, oacc_ref),
                tile_step(jnp.where(i == 0, 0, tpb - i), sts[1], fb_ref, True, obwd_ref)]
        out = [None, None]
        while any(o is None for o in out):
            for n, gen in enumerate(gens):
                if out[n] is None:
                    try:
                        next(gen)
                    except StopIteration as done:
                        out[n] = done.value
        return tuple(out)

    s0 = jnp.zeros((HG_DK, HG_DK), F32)
    lax.fori_loop(0, tpb, both, (s0, s0))

    def readout(t, carry):
        rows = tile_rows(t)
        gate = g_ref[rows, :]
        y = _rms(oacc_ref[rows, :] + obwd_ref[rows, :], gn) * (gate * _sigmoid(gate))
        o_ref[rows, :] = y.astype(o_ref.dtype)
        return carry

    lax.fori_loop(0, tpb, readout, 0)


def _hgrn(hg, lb, gn, B, tpb):
    N = hg.shape[0]
    T = tpb * ROW_TILE
    part = lambda p: pl.BlockSpec((T, HG_DK), lambda b, h: (b, p * HG_HEADS + h))
    return pl.pallas_call(
        functools.partial(_hgrn_kernel, tpb=tpb),
        grid=(B, HG_HEADS),
        in_specs=[part(0), part(1), part(2), part(3), part(4),
                  pl.BlockSpec((None, 1, HG_DK), lambda b, h: (h, 0, 0)),
                  pl.BlockSpec((1, HG_DK), lambda b, h: (0, 0))],
        out_specs=pl.BlockSpec((T, HG_DK), lambda b, h: (b, h)),
        out_shape=jax.ShapeDtypeStruct((N, HG_K), BF16),
        scratch_shapes=[pltpu.VMEM((T, HG_DK), F32), pltpu.VMEM((T, HG_DK), F32)],
        compiler_params=_params("arbitrary", "arbitrary"),
        name="hgrn2",
    )(hg, hg, hg, hg, hg, lb, gn)


def _post_kernel(*refs, n_in, ctx_period):
    xs = refs[:n_in]
    ws = refs[n_in:2 * n_in]
    n_res = 1 if ctx_period is None else 2
    h_refs = refs[2 * n_in + 1:2 * n_in + 1 + n_res]
    bias_ref = refs[2 * n_in]
    (mod_ref, lg_ref, lb_ref, rw_ref, rb_ref,
     h1_out, bin_out, ti_out, tw_out, cnt_out) = refs[2 * n_in + 1 + n_res:]
    i = pl.program_id(0)
    y = bias_ref[...]
    for x_ref, w_ref in zip(xs, ws):
        y = y + jnp.dot(x_ref[...], w_ref[...], preferred_element_type=F32)
    m = mod_ref[...]
    if ctx_period is None:
        h = h_refs[0][...]
    else:
        h = jnp.where(i % ctx_period == 0, h_refs[0][...], h_refs[1][...])
    h1 = _layernorm(DEEPNORM_ALPHA * h + m[2:3] * y, lg_ref[...], lb_ref[...])
    h1_out[...] = h1
    b_in = (h1 * (1.0 + m[4:5]) + m[3:4]).astype(BF16)
    bin_out[...] = b_in
    lg = jnp.dot(b_in, rw_ref[...], preferred_element_type=F32) + rb_ref[...]
    lane = lax.broadcasted_iota(jnp.int32, lg.shape, 1)
    vals, idxs = [], []
    member = jnp.zeros(lg.shape, F32)
    for _ in range(TOP_K):
        mx = jnp.max(lg, axis=-1, keepdims=True)
        ix = jnp.min(jnp.where(lg == mx, lane, LANES), axis=-1, keepdims=True)
        hit = lane == ix
        member = member + hit.astype(F32)
        lg = jnp.where(hit, -jnp.inf, lg)
        vals.append(mx)
        idxs.append(ix)
    es = [jnp.exp(v - vals[0]) for v in vals]
    den = es[0] + es[1] + es[2] + es[3]
    ti = jnp.zeros(lg.shape, jnp.int32)
    tw = jnp.zeros(lg.shape, F32)
    for k in range(TOP_K):
        ti = jnp.where(lane == k, idxs[k], ti)
        tw = jnp.where(lane == k, es[k] / den, tw)
    ti_out[...] = ti
    tw_out[...] = tw

    @pl.when(i == 0)
    def _():
        cnt_out[...] = jnp.zeros(cnt_out.shape, F32)

    cnt_out[0:1, :] += jnp.sum(member, axis=0, keepdims=True)


def _post(xs, ws, bias, hs, h_maps, ctx_period, mods, mod_map, ln_g, ln_b, rw, rb, n_tok):
    D = D_MODEL
    n_in = len(xs)
    const = lambda i: (0, 0)
    row = lambda i: (i, 0)
    in_specs = ([pl.BlockSpec((ROW_TILE, x.shape[1]), row) for x in xs]
                + [pl.BlockSpec(w.shape, const) for w in ws]
                + [pl.BlockSpec((1, D), const)]
                + [pl.BlockSpec((None, ROW_TILE, D), h_map) for h_map in h_maps]
                + [pl.BlockSpec((None, 6, D), mod_map),
                   pl.BlockSpec((1, D), const),
                   pl.BlockSpec((1, D), const),
                   pl.BlockSpec((D, LANES), const),
                   pl.BlockSpec((1, LANES), const)])
    return pl.pallas_call(
        functools.partial(_post_kernel, n_in=n_in, ctx_period=ctx_period),
        grid=(n_tok // ROW_TILE,),
        in_specs=in_specs,
        out_specs=[pl.BlockSpec((ROW_TILE, D), row),
                   pl.BlockSpec((ROW_TILE, D), row),
                   pl.BlockSpec((ROW_TILE, LANES), row),
                   pl.BlockSpec((ROW_TILE, LANES), row),
                   pl.BlockSpec((8, LANES), const)],
        out_shape=[jax.ShapeDtypeStruct((n_tok, D), F32),
                   jax.ShapeDtypeStruct((n_tok, D), BF16),
                   jax.ShapeDtypeStruct((n_tok, LANES), jnp.int32),
                   jax.ShapeDtypeStruct((n_tok, LANES), F32),
                   jax.ShapeDtypeStruct((8, LANES), F32)],
        compiler_params=_params("arbitrary"),
        name="post_mixer",
    )(*xs, *ws, bias, *hs, mods, ln_g, ln_b, rw, rb)


def _plan_kernel(ti_ref, ps_ref, lp_out, cnt_out, goff_out, carry_ref):
    i = pl.program_id(0)

    @pl.when(i == 0)
    def _():
        carry_ref[...] = jnp.zeros(carry_ref.shape, F32)

    ti = ti_ref[...]
    lane = lax.broadcasted_iota(jnp.int32, ti.shape, 1)
    hits = [lane == ti[:, k:k + 1] for k in range(TOP_K)]
    member = hits[0].astype(F32)
    for k in range(1, TOP_K):
        member = member + hits[k].astype(F32)
    row = lax.broadcasted_iota(jnp.int32, (ROW_TILE, ROW_TILE), 0)
    col = lax.broadcasted_iota(jnp.int32, (ROW_TILE, ROW_TILE), 1)
    before = (col < row).astype(BF16)
    rank = jnp.dot(before, member.astype(BF16), preferred_element_type=F32)
    cnt = jnp.sum(member, axis=0, keepdims=True)
    er = lax.broadcasted_iota(jnp.int32, (LANES, LANES), 0)
    ec = lax.broadcasted_iota(jnp.int32, (LANES, LANES), 1)
    cnt8 = jnp.broadcast_to(cnt, (8, LANES))
    loff = jnp.dot(cnt8.astype(BF16), (er < ec).astype(BF16), preferred_element_type=F32)[0:1, :]
    base = rank + loff
    lp = jnp.zeros(ti.shape, F32)
    for k in range(TOP_K):
        pk = jnp.sum(jnp.where(hits[k], base, 0.0), axis=-1, keepdims=True)
        lp = jnp.where(lane == k, pk, lp)
    lp_out[...] = lp.astype(jnp.int32)
    cnt_out[...] = cnt8.astype(jnp.int32)
    goff_out[...] = jnp.broadcast_to(carry_ref[0:1, :] + ps_ref[...], (8, LANES)).astype(jnp.int32)
    carry_ref[0:1, :] += cnt


def _plan(ti, pstart):
    n_tok = ti.shape[0]
    nt = n_tok // ROW_TILE
    return pl.pallas_call(
        _plan_kernel,
        grid=(nt,),
        in_specs=[pl.BlockSpec((ROW_TILE, LANES), lambda i: (i, 0)),
                  pl.BlockSpec((1, LANES), lambda i: (0, 0))],
        out_specs=[pl.BlockSpec((ROW_TILE, LANES), lambda i: (i, 0)),
                   pl.BlockSpec((None, 8, LANES), lambda i: (i, 0, 0)),
                   pl.BlockSpec((None, 8, LANES), lambda i: (i, 0, 0))],
        out_shape=[jax.ShapeDtypeStruct((n_tok, LANES), jnp.int32),
                   jax.ShapeDtypeStruct((nt, 8, LANES), jnp.int32),
                   jax.ShapeDtypeStruct((nt, 8, LANES), jnp.int32)],
        scratch_shapes=[pltpu.VMEM((8, LANES), F32)],
        compiler_params=_params("arbitrary"),
        name="moe_plan",
    )(ti, pstart)


SLOT_ROWS = ROW_TILE * TOP_K
RUN_BITS = ROW_TILE.bit_length()
ROW_SHAPE = (8, LANES)


def _row_copy(src, dst, sem):
    return pltpu.make_async_copy(src, dst, sem)


def _for_each_run_piece(cnt_ref, goff_ref, tile, fn):
    def body(e, lo):
        c = cnt_ref[tile * N_EXPERTS + e]
        g = goff_ref[tile * N_EXPERTS + e]
        for b in range(RUN_BITS):
            size = 1 << b

            @pl.when(jnp.bitwise_and(lax.shift_right_logical(c, b), 1) == 1)
            def _():
                off = jnp.bitwise_and(c, size - 1)
                fn(lo + off, g + off, size)
        return lo + c

    lax.fori_loop(0, N_EXPERTS, body, 0)


def _slot_onehot(lp, n_cols):
    pos = lax.broadcasted_iota(jnp.int32, (lp.shape[0], n_cols), 1)
    hit = pos == lp[:, 0:1]
    for k in range(1, TOP_K):
        hit = jnp.logical_or(hit, pos == lp[:, k:k + 1])
    return hit


def _dispatch_kernel(bv_ref, cnt_ref, goff_ref, lp_ref, x_ref, xs_hbm, zero_ref, buf_ref, sem, zsem):
    n_blocks = xs_hbm.shape[0] // MOE_TILE
    i = pl.program_id(0)
    nt = pl.num_programs(0)
    slot = i % 2

    def wait_slot(s):
        _row_copy(buf_ref.at[s], xs_hbm.at[pl.ds(0, SLOT_ROWS)], sem.at[s]).wait()

    @pl.when(pl.program_id(0) == 0)
    def _():
        zero_ref[...] = jnp.zeros(zero_ref.shape, F32)

        def fill(wait):
            def body(j, carry):
                @pl.when(bv_ref[j] < MOE_TILE)
                def _():
                    rows = pl.ds(pl.multiple_of(j * MOE_TILE, MOE_TILE), MOE_TILE)
                    cp = _row_copy(zero_ref, xs_hbm.at[rows], zsem)
                    if wait:
                        cp.wait()
                    else:
                        cp.start()
                return carry
            return body

        lax.fori_loop(0, n_blocks, fill(False), 0)
        lax.fori_loop(0, n_blocks, fill(True), 0)

    @pl.when(i >= 2)
    def _():
        wait_slot(slot)

    onehot = _slot_onehot(lp_ref[...], SLOT_ROWS).astype(BF16)
    rows = lax.dot_general(onehot, x_ref[...], (((0,), (0,)), ((), ())), preferred_element_type=F32)
    buf_ref[slot] = rows.reshape((SLOT_ROWS,) + ROW_SHAPE)

    def copy_out(lo, g, size):
        _row_copy(buf_ref.at[slot, pl.ds(lo, size)], xs_hbm.at[pl.ds(g, size)], sem.at[slot]).start()

    _for_each_run_piece(cnt_ref, goff_ref, i, copy_out)

    @pl.when(i == nt - 1)
    def _():
        wait_slot(slot)

        @pl.when(nt >= 2)
        def _():
            wait_slot(1 - slot)


def _dispatch(block_valid, cnt_flat, goff_flat, lp, b_in, n_rows):
    n_tok, D = b_in.shape
    grid_spec = pltpu.PrefetchScalarGridSpec(
        num_scalar_prefetch=3,
        grid=(n_tok // ROW_TILE,),
        in_specs=[pl.BlockSpec((ROW_TILE, LANES), lambda i, *_: (i, 0)),
                  pl.BlockSpec((ROW_TILE, D), lambda i, *_: (i, 0))],
        out_specs=pl.BlockSpec(memory_space=pl.ANY),
        scratch_shapes=[pltpu.VMEM((MOE_TILE,) + ROW_SHAPE, F32), pltpu.VMEM((2, SLOT_ROWS) + ROW_SHAPE, F32),
                        pltpu.SemaphoreType.DMA((2,)), pltpu.SemaphoreType.DMA(())],
    )
    return pl.pallas_call(
        _dispatch_kernel,
        grid_spec=grid_spec,
        out_shape=jax.ShapeDtypeStruct((n_rows,) + ROW_SHAPE, F32),
        compiler_params=_params("arbitrary"),
        name="moe_dispatch",
    )(block_valid, cnt_flat, goff_flat, lp, b_in)


GU_GROUP = 2 * LANES


def _pair_split_matrix():
    r = np.arange(GU_GROUP)
    p = np.zeros((GU_GROUP, GU_GROUP), np.float32)
    p[r, np.where(r % 2 == 0, r // 2, LANES + r // 2)] = 1.0
    return jnp.asarray(p, BF16)


def _expert_kernel(be_ref, bv_ref, x_ref, wgu_ref, bgu_ref, wd_ref, bd_ref, split_ref, y_ref, wgu_s, wd_s):
    j = pl.program_id(0)
    valid = bv_ref[j]
    n_groups = wgu_s.shape[1] // GU_GROUP
    changed = jnp.logical_or(j == 0, be_ref[j] != be_ref[jnp.maximum(j - 1, 0)])

    @pl.when(changed)
    def _():
        split = split_ref[...]
        for g in range(n_groups):
            cols = slice(g * GU_GROUP, (g + 1) * GU_GROUP)
            wgu_s[:, cols] = jnp.dot(wgu_ref[:, cols].astype(BF16), split,
                                     preferred_element_type=F32).astype(BF16)
        wd_s[...] = wd_ref[...].astype(BF16)

    @pl.when(valid > 0)
    def _():
        x = x_ref[...].reshape(MOE_TILE, wd_s.shape[1])
        rowi = lax.broadcasted_iota(jnp.int32, x.shape, 0)
        x = jnp.where(rowi < valid, x, 0.0).astype(BF16)
        gu = jnp.dot(x, wgu_s[...], preferred_element_type=F32) + bgu_ref[...]
        acts = []
        for g in range(n_groups):
            gate = jnp.minimum(gu[:, g * GU_GROUP:g * GU_GROUP + LANES], SWIGLU_LIMIT)
            up = jnp.clip(gu[:, g * GU_GROUP + LANES:(g + 1) * GU_GROUP], -SWIGLU_LIMIT, SWIGLU_LIMIT)
            acts.append(((up + 1.0) * gate * _sigmoid(SWIGLU_ALPHA * gate)).astype(BF16))
        act = jnp.concatenate(acts, axis=1)
        y = jnp.dot(act, wd_s[...], preferred_element_type=F32) + bd_ref[...]
        y_ref[...] = y.reshape(y_ref.shape)

    @pl.when(valid <= 0)
    def _():
        y_ref[...] = jnp.zeros(y_ref.shape, F32)


def _experts(block_e, block_valid, xs, layer, w_gu, b_gu_split, w_down, b_down):
    P = xs.shape[0]
    _, E, D, W = w_gu.shape
    DE = w_down.shape[2]
    wmap = lambda j, be, bv: (layer, be[j], 0, 0)
    grid_spec = pltpu.PrefetchScalarGridSpec(
        num_scalar_prefetch=2,
        grid=(P // MOE_TILE,),
        in_specs=[pl.BlockSpec((MOE_TILE,) + ROW_SHAPE, lambda j, be, bv: (j, 0, 0)),
                  pl.BlockSpec((None, None, D, W), wmap),
                  pl.BlockSpec((None, None, 1, W), wmap),
                  pl.BlockSpec((None, None, DE, D), wmap),
                  pl.BlockSpec((None, None, 1, D), wmap),
                  pl.BlockSpec((GU_GROUP, GU_GROUP), lambda j, be, bv: (0, 0))],
        out_specs=pl.BlockSpec((MOE_TILE,) + ROW_SHAPE, lambda j, be, bv: (j, 0, 0)),
        scratch_shapes=[pltpu.VMEM((D, W), BF16), pltpu.VMEM((DE, D), BF16)],
    )
    return pl.pallas_call(
        _expert_kernel,
        grid_spec=grid_spec,
        out_shape=jax.ShapeDtypeStruct((P,) + ROW_SHAPE, F32),
        compiler_params=_params("arbitrary"),
        name="moe_experts",
    )(block_e, block_valid, xs, w_gu, b_gu_split, w_down, b_down, _pair_split_matrix())


def _combine_kernel(cnt_ref, goff_ref, ys_hbm, lp_ref, tw_ref, h_ref, mod_ref, lg_ref, lb_ref, o_ref,
                    buf_ref, sem):
    i = pl.program_id(0)
    nt = pl.num_programs(0)
    slot = i % 2

    def fetch(tile, s):
        def copy_in(lo, g, size):
            _row_copy(ys_hbm.at[pl.ds(g, size)], buf_ref.at[s, pl.ds(lo, size)], sem.at[s]).start()

        _for_each_run_piece(cnt_ref, goff_ref, tile, copy_in)

    @pl.when(i == 0)
    def _():
        fetch(0, 0)

    @pl.when(i + 1 < nt)
    def _():
        fetch(i + 1, 1 - slot)

    _row_copy(ys_hbm.at[pl.ds(0, SLOT_ROWS)], buf_ref.at[slot], sem.at[slot]).wait()

    lp = lp_ref[...]
    tw = tw_ref[...]
    pos = lax.broadcasted_iota(jnp.int32, (ROW_TILE, SLOT_ROWS), 1)
    wsel = jnp.where(pos == lp[:, 0:1], tw[:, 0:1], 0.0)
    for k in range(1, TOP_K):
        wsel = wsel + jnp.where(pos == lp[:, k:k + 1], tw[:, k:k + 1], 0.0)
    w_hi = wsel.astype(BF16)
    w_lo = (wsel - w_hi.astype(F32)).astype(BF16)
    ys = buf_ref[slot].reshape(SLOT_ROWS, h_ref.shape[1]).astype(BF16)
    f = (jnp.dot(w_hi, ys, preferred_element_type=F32) + jnp.dot(w_lo, ys, preferred_element_type=F32))
    m = mod_ref[...]
    o_ref[...] = _layernorm(DEEPNORM_ALPHA * h_ref[...] + m[5:6] * f, lg_ref[...], lb_ref[...])


def _combine(cnt_flat, goff_flat, ys, lp, tw, h1, mods, mod_map, ln_g, ln_b):
    n_tok, D = h1.shape
    const = lambda i, *_: (0, 0)
    row = lambda i, *_: (i, 0)
    grid_spec = pltpu.PrefetchScalarGridSpec(
        num_scalar_prefetch=2,
        grid=(n_tok // ROW_TILE,),
        in_specs=[pl.BlockSpec(memory_space=pl.ANY),
                  pl.BlockSpec((ROW_TILE, LANES), row),
                  pl.BlockSpec((ROW_TILE, LANES), row),
                  pl.BlockSpec((ROW_TILE, D), row),
                  pl.BlockSpec((None, 6, D), lambda i, *_: mod_map(i)),
                  pl.BlockSpec((1, D), const),
                  pl.BlockSpec((1, D), const)],
        out_specs=pl.BlockSpec((ROW_TILE, D), row),
        scratch_shapes=[pltpu.VMEM((2, SLOT_ROWS) + ROW_SHAPE, F32), pltpu.SemaphoreType.DMA((2,))],
    )
    return pl.pallas_call(
        _combine_kernel,
        grid_spec=grid_spec,
        out_shape=jax.ShapeDtypeStruct((n_tok, D), F32),
        compiler_params=_params("arbitrary"),
        name="moe_combine",
    )(cnt_flat, goff_flat, ys, lp, tw, h1, mods, ln_g, ln_b)


def _moe(b_in, ti, tw, counts, h1, mods, mod_map, ln_g, ln_b, layer, w_gu, b_gu, w_down, b_down):
    n_tok = b_in.shape[0]
    cnt = counts[0, :N_EXPERTS].astype(jnp.int32)
    pcnt = (cnt + MOE_TILE - 1) // MOE_TILE * MOE_TILE
    pends = jnp.cumsum(pcnt)
    pstart = pends - pcnt
    n_blocks = -(-(n_tok * TOP_K + N_EXPERTS * (MOE_TILE - 1)) // MOE_TILE)
    blk0 = jnp.arange(n_blocks, dtype=jnp.int32) * MOE_TILE
    block_e = jnp.minimum(jnp.sum((pends[None, :] <= blk0[:, None]).astype(jnp.int32), axis=1), N_EXPERTS - 1)
    block_valid = jnp.clip(cnt[block_e] - (blk0 - pstart[block_e]), 0, MOE_TILE).astype(jnp.int32)
    ps_row = jnp.zeros((1, LANES), F32).at[0, :N_EXPERTS].set(pstart.astype(F32))
    lp, cnt_t, goff_t = _plan(ti, ps_row)
    cnt_flat = cnt_t[:, 0, :N_EXPERTS].reshape(-1)
    goff_flat = goff_t[:, 0, :N_EXPERTS].reshape(-1)
    xs = _dispatch(block_valid, cnt_flat, goff_flat, lp, b_in, n_blocks * MOE_TILE)
    L, E, W = b_gu.shape
    b_split = b_gu.reshape(L, E, W // GU_GROUP, LANES, 2).transpose(0, 1, 2, 4, 3).reshape(L, E, 1, W)
    ys = _experts(block_e, block_valid, xs, layer, w_gu, b_split, w_down, b_down.reshape(L, E, 1, -1))
    return _combine(cnt_flat, goff_flat, ys, lp, tw, h1, mods, mod_map, ln_g, ln_b)


def _router_weights(router_w, router_b):
    D = router_w.shape[0]
    rw = jnp.zeros((D, LANES), F32).at[:, :N_EXPERTS].set(router_w).astype(BF16)
    rb = jnp.full((1, LANES), -jnp.inf, F32).at[0, :N_EXPERTS].set(router_b)
    return rw, rb


def _glu_kernel(h_ref, mod_ref, w_ref, b_ref, u_out):
    m = mod_ref[...]
    a = (h_ref[...] * (1.0 + m[1:2]) + m[0:1]).astype(BF16)
    z = jnp.dot(a, w_ref[...], preferred_element_type=F32) + b_ref[...]
    half = z.shape[1] // 2
    u_out[...] = z[:, :half] * _sigmoid(z[:, half:])


def _glu(h3, mods, w, b, B, tpb_lat):
    D = D_MODEL
    return pl.pallas_call(
        _glu_kernel,
        grid=(B, tpb_lat),
        in_specs=[pl.BlockSpec((None, ROW_TILE, D), lambda b, j: (b, j + 1, 0)),
                  pl.BlockSpec((None, 6, D), lambda b, j: (b, 0, 0)),
                  pl.BlockSpec(w.shape, lambda b, j: (0, 0)),
                  pl.BlockSpec((1, w.shape[1]), lambda b, j: (0, 0))],
        out_specs=pl.BlockSpec((None, ROW_TILE, D), lambda b, j: (b, j, 0)),
        out_shape=jax.ShapeDtypeStruct((B, tpb_lat * ROW_TILE, D), F32),
        compiler_params=_params("arbitrary", "arbitrary"),
        name="conv_glu",
    )(h3, mods, w, b)


def _dwconv_kernel(prev_ref, cur_ref, next_ref, w_ref, b_ref, lg_ref, lb_ref, o_ref, win_ref, shift_ref, *,
                   tpb_lat):
    j = pl.program_id(1)
    H = CONV_HALO
    zero = jnp.zeros((H, cur_ref.shape[1]), F32)
    win_ref[0:H, :] = jnp.where(j > 0, prev_ref[...], zero)
    win_ref[H:H + ROW_TILE, :] = cur_ref[...]
    win_ref[H + ROW_TILE:, :] = jnp.where(j < tpb_lat - 1, next_ref[...], zero)
    w = w_ref[...]
    off = H - CONV_WIDTH // 2
    acc = jnp.zeros(cur_ref.shape, F32) + b_ref[...]
    sub = ROW_SHAPE[0]
    for r in range(sub):
        taps = [k for k in range(CONV_WIDTH) if (off + k) % sub == r]
        if not taps:
            continue
        src = win_ref
        if r:
            span = (off + taps[-1]) // sub * sub + ROW_TILE
            shift_ref[r % 2, 0:span, :] = win_ref[r:r + span, :]
            src = shift_ref.at[r % 2]
        for k in taps:
            a0 = (off + k) // sub * sub
            acc = acc + src[a0:a0 + ROW_TILE, :] * w[k:k + 1, :]
    y = _layernorm(acc, lg_ref[...], lb_ref[...])
    o_ref[...] = (y * _sigmoid(y)).astype(o_ref.dtype)


def _dwconv(u, w, b, ln_g, ln_b, B, tpb_lat):
    D = D_MODEL
    n = tpb_lat * ROW_TILE
    per = ROW_TILE // CONV_HALO
    last = n // CONV_HALO - 1
    const = lambda b_, j: (0, 0)
    return pl.pallas_call(
        functools.partial(_dwconv_kernel, tpb_lat=tpb_lat),
        grid=(B, tpb_lat),
        in_specs=[pl.BlockSpec((None, CONV_HALO, D), lambda b_, j: (b_, jnp.maximum(j * per - 1, 0), 0)),
                  pl.BlockSpec((None, ROW_TILE, D), lambda b_, j: (b_, j, 0)),
                  pl.BlockSpec((None, CONV_HALO, D), lambda b_, j: (b_, jnp.minimum((j + 1) * per, last), 0)),
                  pl.BlockSpec(w.shape, const),
                  pl.BlockSpec((1, D), const),
                  pl.BlockSpec((1, D), const),
                  pl.BlockSpec((1, D), const)],
        out_specs=pl.BlockSpec((ROW_TILE, D), lambda b_, j: (b_ * tpb_lat + j, 0)),
        out_shape=jax.ShapeDtypeStruct((B * n, D), BF16),
        scratch_shapes=[pltpu.VMEM((ROW_TILE + 2 * CONV_HALO, D), F32),
                        pltpu.VMEM((2, ROW_TILE + 2 * CONV_HALO, D), F32)],
        compiler_params=_params("arbitrary", "arbitrary"),
        name="conv_dw",
    )(u, u, u, w, b, ln_g, ln_b)


def kernel(x, c, ctx, c_ctx, mod_w, mod_b, ln1_g, ln1_b, ln2_g, ln2_b, mix_w_in, mla_q_norm, mla_kv_norm, mla_w_uq, mla_w_uk, mla_w_uv, hg_lb_logits, hg_norm, mix_w_out, conv_pw1_w, conv_pw1_b, conv_dw_w, conv_dw_b, conv_ln_g, conv_ln_b, conv_pw2_w, conv_pw2_b, router_w, router_b, exp_w_gu, exp_b_gu, exp_w_down, exp_b_down):
    B, n, D = x.shape
    tc = ctx.shape[1]
    assert D == D_MODEL and tc == ROW_TILE and n % ROW_TILE == 0 and n % GRID_W == 0
    assert B + 1 <= 16
    T = tc + n
    tpb = T // ROW_TILE
    tpb_lat = n // ROW_TILE
    N = B * T
    row2 = lambda v: v.reshape(1, -1)

    cond = jnp.zeros((16, D), F32).at[:B].set(c).at[B].set(c_ctx)
    mods = _modulation(cond, mod_w, mod_b)[:, :B + 1].reshape(DEPTH, B + 1, 6, D)
    lb_all = jnp.cumsum(jax.nn.softmax(hg_lb_logits.astype(F32), axis=0), axis=0)

    win, wq2, wkv = _inproj_weights(mix_w_in[0], mla_w_uq[0], mla_w_uk[0], mla_w_uv[0])
    cos_t, sin_t = _rope_tables(n, tc)
    q, k, v, hg = _inproj(ctx, x, mods[0], win, row2(mla_q_norm[0]), row2(mla_kv_norm[0]),
                          wq2, wkv, cos_t, sin_t, B, tpb)
    att = _attention(q, k, v, B, tpb, tc)
    hgo = _hgrn(hg, lb_all[0].reshape(HG_HEADS, 1, HG_DK), row2(hg_norm[0]), B, tpb)
    w_out = mix_w_out[0].astype(BF16)
    n_att = MLA_HEADS * MLA_V
    rw, rb = _router_weights(router_w[0], router_b[0])
    mod_map0 = lambda i: (jnp.where(i % tpb == 0, B, i // tpb), 0, 0)
    h1, b_in, ti, tw, counts = _post(
        [att, hgo], [w_out[:n_att], w_out[n_att:]], jnp.zeros((1, D), F32),
        [ctx, x], [_ctx_map(tpb), _lat_map(tpb)], tpb, mods[0], mod_map0,
        row2(ln1_g[0]), row2(ln1_b[0]), rw, rb, N)
    ew = (exp_w_gu, exp_b_gu, exp_w_down, exp_b_down)
    h2 = _moe(b_in, ti, tw, counts, h1, mods[0], mod_map0, row2(ln2_g[0]), row2(ln2_b[0]), 0, *ew)

    n_lat = B * n
    h2_3 = h2.reshape(B * tpb, ROW_TILE, D)
    u = _glu(h2.reshape(B, T, D), mods[1], conv_pw1_w[0].astype(BF16), row2(conv_pw1_b[0]), B, tpb_lat)
    cv = _dwconv(u, conv_dw_w[0], row2(conv_dw_b[0]), row2(conv_ln_g[0]), row2(conv_ln_b[0]), B, tpb_lat)
    rw, rb = _router_weights(router_w[1], router_b[1])
    mod_map1 = lambda i: (i // tpb_lat, 0, 0)
    h_map1 = lambda i: ((i // tpb_lat) * tpb + i % tpb_lat + 1, 0, 0)
    h3, b_in, ti, tw, counts = _post(
        [cv], [conv_pw2_w[0].astype(BF16)], row2(conv_pw2_b[0]),
        [h2_3], [h_map1], None, mods[1], mod_map1, row2(ln1_g[1]), row2(ln1_b[1]), rw, rb, n_lat)
    h4 = _moe(b_in, ti, tw, counts, h3, mods[1], mod_map1, row2(ln2_g[1]), row2(ln2_b[1]), 1, *ew)
    return h4.reshape(B, n, D)
```

```python
import functools

import numpy as np
import jax
import jax.numpy as jnp
from jax import lax
from jax.experimental import pallas as pl
from jax.experimental.pallas import tpu as pltpu

F32 = jnp.float32
BF16 = jnp.bfloat16

D_MODEL = 1024
DEPTH = 2
GRID_W = 64
MLA_HEADS = 8
MLA_Q_LORA = 384
MLA_KV_LORA = 256
MLA_NOPE = 64
MLA_ROPE = 32
MLA_V = 64
MLA_SCALE = (MLA_NOPE + MLA_ROPE) ** -0.5
LOG2_E = 1.4426950408889634
ROPE_BASE = 10000.0
HG_HEADS = 4
HG_DK = 128
HG_CHUNK = 32
HG_K = HG_HEADS * HG_DK
CONV_WIDTH = 31
N_EXPERTS = 32
TOP_K = 4
SWIGLU_LIMIT = 7.0
SWIGLU_ALPHA = 1.702
LN_EPS = 1e-5
RMS_EPS = 1e-6
DEEPNORM_ALPHA = (2.0 * DEPTH) ** 0.25

LANES = 128
ROW_TILE = 256
HEAD_BLOCK = 128
KEY_CHUNK = 1024
MOE_TILE = 512
CONV_HALO = 16
IN_WIDTH = 3456
HG_PARTS = 5 * HG_K
VMEM_LIMIT = 56 * 1024 * 1024


def _params(*sem):
    return pltpu.CompilerParams(dimension_semantics=sem, vmem_limit_bytes=VMEM_LIMIT)


def _sigmoid(x):
    return 1.0 / (1.0 + jnp.exp(-x))


def _layernorm(x, g, b):
    xc = x - jnp.mean(x, axis=-1, keepdims=True)
    var = jnp.mean(xc * xc, axis=-1, keepdims=True)
    return xc * lax.rsqrt(var + LN_EPS) * g + b


def _rms(x, g):
    return x * lax.rsqrt(jnp.mean(x * x, axis=-1, keepdims=True) + RMS_EPS) * g


def _mod_kernel(c_ref, w_ref, b_ref, o_ref):
    c = c_ref[...]
    s = (c * _sigmoid(c)).astype(BF16)
    o_ref[...] = jnp.dot(s, w_ref[...].astype(BF16), preferred_element_type=F32) + b_ref[...]


def _modulation(cond, mod_w, mod_b):
    L, D, W = mod_w.shape
    R = cond.shape[0]
    return pl.pallas_call(
        _mod_kernel,
        grid=(L, W // D),
        in_specs=[pl.BlockSpec((R, D), lambda l, n: (0, 0)),
                  pl.BlockSpec((None, D, D), lambda l, n: (l, 0, n)),
                  pl.BlockSpec((None, 1, D), lambda l, n: (l, 0, n))],
        out_specs=pl.BlockSpec((None, R, D), lambda l, n: (l, 0, n)),
        out_shape=jax.ShapeDtypeStruct((L, R, W), F32),
        compiler_params=_params("arbitrary", "arbitrary"),
        name="adaln_mod",
    )(cond, mod_w, mod_b.reshape(L, 1, W))


def _inproj_kernel(ctx_ref, x_ref, mod_ref, win_ref, qn_ref, kvn_ref, wq_ref, wkv_ref, c_ref, s_ref,
                   q_out, k_out, v_out, hg_out, *, tpb):
    m = mod_ref[...]
    h = jnp.where(pl.program_id(0) % tpb == 0, ctx_ref[...], x_ref[...])
    a = (h * (1.0 + m[1:2]) + m[0:1]).astype(BF16)
    z = jnp.dot(a, win_ref[...], preferred_element_type=F32)
    cos = c_ref[...]
    sin = s_ref[...]
    cos_h = jnp.concatenate([cos] * MLA_HEADS, axis=1)
    sin_h = jnp.concatenate([sin] * MLA_HEADS, axis=1)
    width = MLA_HEADS * HEAD_BLOCK

    cqn = _rms(z[:, :MLA_Q_LORA], qn_ref[...]).astype(BF16)
    qq = jnp.dot(cqn, wq_ref[...], preferred_element_type=F32)
    q_out[...] = ((qq[:, :width] * cos_h + qq[:, width:] * sin_h) * (MLA_SCALE * LOG2_E)).astype(BF16)

    c0 = MLA_Q_LORA + MLA_KV_LORA
    ckvn = _rms(z[:, MLA_Q_LORA:c0], kvn_ref[...]).astype(BF16)
    kv = jnp.dot(ckvn, wkv_ref[...], preferred_element_type=F32)
    r0 = c0 + HG_PARTS
    kr = z[:, r0:r0 + HEAD_BLOCK] * cos + z[:, r0 + HEAD_BLOCK:r0 + 2 * HEAD_BLOCK] * sin
    k_out[...] = (kv[:, :width] + jnp.concatenate([kr] * MLA_HEADS, axis=1)).astype(BF16)
    v_out[...] = kv[:, width:].astype(BF16)
    hg_out[...] = z[:, c0:r0]


def _rope_partner(w):
    shp = w.shape
    wr = w.reshape(shp[:-1] + (shp[-1] // 2, 2))
    return jnp.stack([-wr[..., 1], wr[..., 0]], axis=-1).reshape(shp)


def _inproj_weights(w_in, w_uq, w_uk, w_uv):
    D = w_in.shape[0]
    o_kr = MLA_Q_LORA + MLA_KV_LORA
    w_kr = w_in[:, o_kr:o_kr + MLA_ROPE]
    z_nope = jnp.zeros((D, MLA_NOPE), F32)
    z_pad = jnp.zeros((D, HEAD_BLOCK - MLA_NOPE - MLA_ROPE), F32)
    krb = jnp.concatenate([z_nope, w_kr, z_pad], axis=1)
    krb_p = jnp.concatenate([z_nope, _rope_partner(w_kr), z_pad], axis=1)
    win = jnp.concatenate([w_in[:, :o_kr], w_in[:, o_kr + MLA_ROPE:], krb, krb_p], axis=1).astype(BF16)

    R = w_uq.shape[0]
    wq = w_uq.reshape(R, MLA_HEADS, MLA_NOPE + MLA_ROPE)
    nope, rope = wq[..., :MLA_NOPE], wq[..., MLA_NOPE:]
    zp = jnp.zeros((R, MLA_HEADS, HEAD_BLOCK - MLA_NOPE - MLA_ROPE), F32)
    zn = jnp.zeros((R, MLA_HEADS, MLA_NOPE), F32)
    q_main = jnp.concatenate([nope, rope, zp], axis=-1).reshape(R, MLA_HEADS * HEAD_BLOCK)
    q_part = jnp.concatenate([zn, _rope_partner(rope), zp], axis=-1).reshape(R, MLA_HEADS * HEAD_BLOCK)
    wq2 = jnp.concatenate([q_main, q_part], axis=1).astype(BF16)

    C = w_uk.shape[0]
    wk = w_uk.reshape(C, MLA_HEADS, MLA_NOPE)
    wk = jnp.concatenate([wk, jnp.zeros((C, MLA_HEADS, HEAD_BLOCK - MLA_NOPE), F32)], axis=-1)
    wv = w_uv.reshape(C, MLA_HEADS // 2, 2, MLA_V)
    zv = jnp.zeros((C, MLA_HEADS // 2, MLA_V), F32)
    wv = jnp.stack([jnp.concatenate([wv[:, :, 0], zv], -1), jnp.concatenate([zv, wv[:, :, 1]], -1)], axis=2)
    wkv = jnp.concatenate([wk.reshape(C, -1), wv.reshape(C, -1)], axis=1).astype(BF16)
    return win, wq2, wkv


def _rope_tables(n, tc):
    rows = n // GRID_W
    pos_r = jnp.repeat(jnp.arange(rows, dtype=F32), GRID_W)
    pos_c = jnp.tile(jnp.arange(GRID_W, dtype=F32), rows)
    half = MLA_ROPE // 2
    inv = jnp.power(ROPE_BASE, -jnp.arange(0, half, 2, dtype=F32) / half)
    ang = jnp.concatenate([pos_r[:, None] * inv, pos_c[:, None] * inv], axis=-1)
    cos2 = jnp.repeat(jnp.cos(ang), 2, axis=-1)
    sin2 = jnp.repeat(jnp.sin(ang), 2, axis=-1)
    pad = HEAD_BLOCK - MLA_NOPE - MLA_ROPE
    cos_l = jnp.concatenate([jnp.ones((n, MLA_NOPE), F32), cos2, jnp.ones((n, pad), F32)], axis=1)
    sin_l = jnp.concatenate([jnp.zeros((n, MLA_NOPE), F32), sin2, jnp.zeros((n, pad), F32)], axis=1)
    cos_t = jnp.concatenate([jnp.ones((tc, HEAD_BLOCK), F32), cos_l], axis=0)
    sin_t = jnp.concatenate([jnp.zeros((tc, HEAD_BLOCK), F32), sin_l], axis=0)
    return cos_t, sin_t


def _ctx_map(tpb):
    return lambda i: (i // tpb, 0, 0)


def _lat_map(tpb):
    return lambda i: (i // tpb, jnp.maximum(i % tpb - 1, 0), 0)


def _inproj(ctx, x, mods, win, qn, kvn, wq2, wkv, cos_t, sin_t, B, tpb):
    D = x.shape[2]
    N = B * tpb * ROW_TILE
    ctx_row = mods.shape[0] - 1
    width = MLA_HEADS * HEAD_BLOCK
    const = lambda i: (0, 0)
    return pl.pallas_call(
        functools.partial(_inproj_kernel, tpb=tpb),
        grid=(N // ROW_TILE,),
        in_specs=[pl.BlockSpec((None, ROW_TILE, D), _ctx_map(tpb)),
                  pl.BlockSpec((None, ROW_TILE, D), _lat_map(tpb)),
                  pl.BlockSpec((None, 6, D), lambda i: (jnp.where(i % tpb == 0, ctx_row, i // tpb), 0, 0)),
                  pl.BlockSpec(win.shape, const),
                  pl.BlockSpec(qn.shape, const),
                  pl.BlockSpec(kvn.shape, const),
                  pl.BlockSpec(wq2.shape, const),
                  pl.BlockSpec(wkv.shape, const),
                  pl.BlockSpec((ROW_TILE, HEAD_BLOCK), lambda i: (i % tpb, 0)),
                  pl.BlockSpec((ROW_TILE, HEAD_BLOCK), lambda i: (i % tpb, 0))],
        out_specs=[pl.BlockSpec((ROW_TILE, width), lambda i: (i, 0)),
                   pl.BlockSpec((ROW_TILE, width), lambda i: (i, 0)),
                   pl.BlockSpec((ROW_TILE, width), lambda i: (i, 0)),
                   pl.BlockSpec((ROW_TILE, HG_PARTS), lambda i: (i, 0))],
        out_shape=[jax.ShapeDtypeStruct((N, width), BF16),
                   jax.ShapeDtypeStruct((N, width), BF16),
                   jax.ShapeDtypeStruct((N, width), BF16),
                   jax.ShapeDtypeStruct((N, HG_PARTS), F32)],
        compiler_params=_params("arbitrary"),
        name="inproj",
    )(ctx, x, mods, win, qn, kvn, wq2, wkv, cos_t, sin_t)


ATTN_HEADS = 4
SCORE_LOOKAHEAD = 1


def _attn_kernel(q_ref, k_ref, v_ref, o_ref, *, chunks):
    j = pl.program_id(2)

    def scores(hh, n_chunks):
        lo, hi = hh * HEAD_BLOCK, (hh + 1) * HEAD_BLOCK
        q = q_ref[:, lo:hi]
        ss, m = [], None
        for s0, s1 in chunks[:n_chunks]:
            s = lax.dot_general(q, k_ref[s0:s1, lo:hi], (((1,), (1,)), ((), ())), preferred_element_type=F32)
            mc = jnp.max(s, axis=-1, keepdims=True)
            m = mc if m is None else jnp.maximum(m, mc)
            ss.append(s)
        return ss, m

    def softmax_values(hh, scored, n_chunks):
        lo, hi = hh * HEAD_BLOCK, (hh + 1) * HEAD_BLOCK
        ss, m = scored
        l = acc = None
        for s, (s0, s1) in zip(ss, chunks[:n_chunks]):
            p = jnp.exp2(s - m)
            lc = jnp.sum(p, axis=-1, keepdims=True)
            ac = jnp.dot(p.astype(BF16), v_ref[s0:s1, lo:hi], preferred_element_type=F32)
            l = lc if l is None else l + lc
            acc = ac if acc is None else acc + ac
        return acc / l

    def pairs(n_chunks):
        heads = [None] * ATTN_HEADS
        ss = [scores(hh, n_chunks) for hh in range(min(SCORE_LOOKAHEAD, ATTN_HEADS))]
        for hh in range(ATTN_HEADS):
            if hh + SCORE_LOOKAHEAD < ATTN_HEADS:
                ss.append(scores(hh + SCORE_LOOKAHEAD, n_chunks))
            heads[hh] = softmax_values(hh, ss[hh], n_chunks)
            ss[hh] = None
        outs = [heads[hh] + heads[hh + 1] for hh in range(0, ATTN_HEADS, 2)]
        o_ref[...] = jnp.concatenate(outs, axis=1).astype(o_ref.dtype)

    @pl.when(j == 0)
    def _():
        pairs(1)

    @pl.when(j > 0)
    def _():
        pairs(len(chunks))


def _attention(q, k, v, B, tpb, tc):
    N = q.shape[0]
    T = tpb * ROW_TILE
    kc = min(KEY_CHUNK, T - tc)
    chunks = ((0, tc),) + tuple((s, s + kc) for s in range(tc, T, kc))
    width = ATTN_HEADS * HEAD_BLOCK
    return pl.pallas_call(
        functools.partial(_attn_kernel, chunks=chunks),
        grid=(B, MLA_HEADS // ATTN_HEADS, tpb),
        in_specs=[pl.BlockSpec((ROW_TILE, width), lambda b, p, j: (b * tpb + j, p)),
                  pl.BlockSpec((T, width), lambda b, p, j: (b, p)),
                  pl.BlockSpec((T, width), lambda b, p, j: (b, p))],
        out_specs=pl.BlockSpec((ROW_TILE, ATTN_HEADS * MLA_V), lambda b, p, j: (b * tpb + j, p)),
        out_shape=jax.ShapeDtypeStruct((N, MLA_HEADS * MLA_V), BF16),
        compiler_params=_params("arbitrary", "arbitrary", "arbitrary"),
        name="mla_attention",
    )(q, k, v)


def _hgrn_kernel(q_ref, ff_ref, fb_ref, v_ref, g_ref, lb_ref, gn_ref, o_ref, oacc_ref, obwd_ref, *, tpb):
    lb = lb_ref[...]
    gn = gn_ref[...]
    C = HG_CHUNK
    per_tile = ROW_TILE // C
    row = lax.broadcasted_iota(jnp.int32, (ROW_TILE, ROW_TILE), 0)
    col = lax.broadcasted_iota(jnp.int32, (ROW_TILE, ROW_TILE), 1)
    same = (row // C) == (col // C)
    ones_blk = same.astype(BF16)
    row_chunk = lax.broadcasted_iota(jnp.int32, (ROW_TILE, HG_DK), 0) // C
    nt_dims = (((1,), (1,)), ((), ()))
    tn_dims = (((0,), (0,)), ((), ()))

    def tile_rows(t):
        return pl.ds(pl.multiple_of(t * ROW_TILE, ROW_TILE), ROW_TILE)

    def chunk_sums(m01, g3):
        return sum(jnp.dot(m01, part, preferred_element_type=F32) for part in g3)

    def per_chunk_blocks(x):
        zero = jnp.zeros(x.shape, x.dtype)
        return jnp.concatenate([jnp.where(row_chunk == c, x, zero) for c in range(per_tile)], axis=1)

    def tile_step(t, st, f_ref, reverse, acc_ref):
        rows = tile_rows(t)
        f = lb + (1.0 - lb) * _sigmoid(f_ref[rows, :])
        kk = 1.0 - f
        g = jnp.log(f)
        g_hi = g.astype(BF16)
        r1 = g - g_hi.astype(F32)
        g_mid = r1.astype(BF16)
        g3 = (g_hi, g_mid, (r1 - g_mid.astype(F32)).astype(BF16))
        in_chunk = same & ((col >= row) if reverse else (col <= row))
        bcum = chunk_sums(in_chunk.astype(BF16), g3)
        btot = chunk_sums(ones_blk, g3)
        yield
        qin = (q_ref[rows, :] * jnp.exp(bcum)).astype(BF16)
        kdec = (kk * jnp.exp(-bcum)).astype(BF16)
        krem = (kk * jnp.exp(btot - bcum)).astype(BF16)
        dec = jnp.exp(btot)
        v = v_ref[rows, :].astype(BF16)
        a = lax.dot_general(qin, kdec, nt_dims, preferred_element_type=F32)
        ut = lax.dot_general(v, per_chunk_blocks(krem), tn_dims, preferred_element_type=F32)
        yield
        a = jnp.where(in_chunk, a, 0.0).astype(BF16)
        o_t = jnp.dot(a, v, preferred_element_type=F32)
        states = [None] * per_tile
        for c in (range(per_tile - 1, -1, -1) if reverse else range(per_tile)):
            states[c] = st.astype(BF16)
            st = st * dec[c * C:c * C + 1, :] + ut[:, c * HG_DK:(c + 1) * HG_DK]
        yield
        o_t = o_t + lax.dot_general(per_chunk_blocks(qin), jnp.concatenate(states, axis=1), nt_dims,
                                    preferred_element_type=F32)
        acc_ref[rows, :] = o_t
        return st

    def both(i, sts):
        gens = [tile_step(i, sts[0], ff_ref, False, oacc_ref),
                tile_step(jnp.where(i == 0, 0, tpb - i), sts[1], fb_ref, True, obwd_ref)]
        out = [None, None]
        while any(o is None for o in out):
            for n, gen in enumerate(gens):
                if out[n] is None:
                    try:
                        next(gen)
                    except StopIteration as done:
                        out[n] = done.value
        return tuple(out)

    s0 = jnp.zeros((HG_DK, HG_DK), F32)
    lax.fori_loop(0, tpb, both, (s0, s0))

    def readout(t, carry):
        rows = tile_rows(t)
        gate = g_ref[rows, :]
        y = _rms(oacc_ref[rows, :] + obwd_ref[rows, :], gn) * (gate * _sigmoid(gate))
        o_ref[rows, :] = y.astype(o_ref.dtype)
        return carry

    lax.fori_loop(0, tpb, readout, 0)


def _hgrn(hg, lb, gn, B, tpb):
    N = hg.shape[0]
    T = tpb * ROW_TILE
    part = lambda p: pl.BlockSpec((T, HG_DK), lambda b, h: (b, p * HG_HEADS + h))
    return pl.pallas_call(
        functools.partial(_hgrn_kernel, tpb=tpb),
        grid=(B, HG_HEADS),
        in_specs=[part(0), part(1), part(2), part(3), part(4),
                  pl.BlockSpec((None, 1, HG_DK), lambda b, h: (h, 0, 0)),
                  pl.BlockSpec((1, HG_DK), lambda b, h: (0, 0))],
        out_specs=pl.BlockSpec((T, HG_DK), lambda b, h: (b, h)),
        out_shape=jax.ShapeDtypeStruct((N, HG_K), BF16),
        scratch_shapes=[pltpu.VMEM((T, HG_DK), F32), pltpu.VMEM((T, HG_DK), F32)],
        compiler_params=_params("arbitrary", "arbitrary"),
        name="hgrn2",
    )(hg, hg, hg, hg, hg, lb, gn)


def _post_kernel(*refs, n_in, ctx_period):
    xs = refs[:n_in]
    ws = refs[n_in:2 * n_in]
    n_res = 1 if ctx_period is None else 2
    h_refs = refs[2 * n_in + 1:2 * n_in + 1 + n_res]
    bias_ref = refs[2 * n_in]
    (mod_ref, lg_ref, lb_ref, rw_ref, rb_ref,
     h1_out, bin_out, lp_out, tw_out, cnt_out, run_out, carry_out) = refs[2 * n_in + 1 + n_res:]
    i = pl.program_id(0)
    y = bias_ref[...]
    for x_ref, w_ref in zip(xs, ws):
        y = y + jnp.dot(x_ref[...], w_ref[...], preferred_element_type=F32)
    m = mod_ref[...]
    if ctx_period is None:
        h = h_refs[0][...]
    else:
        h = jnp.where(i % ctx_period == 0, h_refs[0][...], h_refs[1][...])
    h1 = _layernorm(DEEPNORM_ALPHA * h + m[2:3] * y, lg_ref[...], lb_ref[...])
    h1_out[...] = h1
    b_in = (h1 * (1.0 + m[4:5]) + m[3:4]).astype(BF16)
    bin_out[...] = b_in
    lg = jnp.dot(b_in, rw_ref[...], preferred_element_type=F32) + rb_ref[...]
    lane = lax.broadcasted_iota(jnp.int32, lg.shape, 1)
    vals, hits = [], []
    member = jnp.zeros(lg.shape, F32)
    for _ in range(TOP_K):
        mx = jnp.max(lg, axis=-1, keepdims=True)
        ix = jnp.min(jnp.where(lg == mx, lane, LANES), axis=-1, keepdims=True)
        hit = lane == ix
        member = member + hit.astype(F32)
        lg = jnp.where(hit, -jnp.inf, lg)
        vals.append(mx)
        hits.append(hit)
    es = [jnp.exp(v - vals[0]) for v in vals]
    den = es[0] + es[1] + es[2] + es[3]

    row = lax.broadcasted_iota(jnp.int32, (ROW_TILE, ROW_TILE), 0)
    col = lax.broadcasted_iota(jnp.int32, (ROW_TILE, ROW_TILE), 1)
    rank = jnp.dot((col < row).astype(BF16), member.astype(BF16), preferred_element_type=F32)
    cnt = jnp.sum(member, axis=0, keepdims=True)
    er = lax.broadcasted_iota(jnp.int32, (LANES, LANES), 0)
    ec = lax.broadcasted_iota(jnp.int32, (LANES, LANES), 1)
    cnt8 = jnp.broadcast_to(cnt, (8, LANES))
    loff = jnp.dot(cnt8.astype(BF16), (er < ec).astype(BF16), preferred_element_type=F32)[0:1, :]
    base = rank + loff
    lp = jnp.zeros(lg.shape, F32)
    tw = jnp.zeros(lg.shape, F32)
    for k in range(TOP_K):
        pk = jnp.sum(jnp.where(hits[k], base, 0.0), axis=-1, keepdims=True)
        lp = jnp.where(lane == k, pk, lp)
        tw = jnp.where(lane == k, es[k] / den, tw)
    lp_out[...] = lp.astype(jnp.int32)
    tw_out[...] = tw
    run_out[...] = cnt8.astype(jnp.int32)

    @pl.when(i == 0)
    def _():
        cnt_out[...] = jnp.zeros(cnt_out.shape, F32)

    carry_out[...] = jnp.broadcast_to(cnt_out[0:1, :], (8, LANES)).astype(jnp.int32)
    cnt_out[0:1, :] += cnt


def _post(xs, ws, bias, hs, h_maps, ctx_period, mods, mod_map, ln_g, ln_b, rw, rb, n_tok):
    D = D_MODEL
    n_in = len(xs)
    const = lambda i: (0, 0)
    row = lambda i: (i, 0)
    in_specs = ([pl.BlockSpec((ROW_TILE, x.shape[1]), row) for x in xs]
                + [pl.BlockSpec(w.shape, const) for w in ws]
                + [pl.BlockSpec((1, D), const)]
                + [pl.BlockSpec((None, ROW_TILE, D), h_map) for h_map in h_maps]
                + [pl.BlockSpec((None, 6, D), mod_map),
                   pl.BlockSpec((1, D), const),
                   pl.BlockSpec((1, D), const),
                   pl.BlockSpec((D, LANES), const),
                   pl.BlockSpec((1, LANES), const)])
    nt = n_tok // ROW_TILE
    per_tile = pl.BlockSpec((None, 8, LANES), lambda i: (i, 0, 0))
    return pl.pallas_call(
        functools.partial(_post_kernel, n_in=n_in, ctx_period=ctx_period),
        grid=(nt,),
        in_specs=in_specs,
        out_specs=[pl.BlockSpec((ROW_TILE, D), row),
                   pl.BlockSpec((ROW_TILE, D), row),
                   pl.BlockSpec((ROW_TILE, LANES), row),
                   pl.BlockSpec((ROW_TILE, LANES), row),
                   pl.BlockSpec((8, LANES), const),
                   per_tile,
                   per_tile],
        out_shape=[jax.ShapeDtypeStruct((n_tok, D), F32),
                   jax.ShapeDtypeStruct((n_tok, D), BF16),
                   jax.ShapeDtypeStruct((n_tok, LANES), jnp.int32),
                   jax.ShapeDtypeStruct((n_tok, LANES), F32),
                   jax.ShapeDtypeStruct((8, LANES), F32),
                   jax.ShapeDtypeStruct((nt, 8, LANES), jnp.int32),
                   jax.ShapeDtypeStruct((nt, 8, LANES), jnp.int32)],
        compiler_params=_params("arbitrary"),
        name="post_mixer",
    )(*xs, *ws, bias, *hs, mods, ln_g, ln_b, rw, rb)


SLOT_ROWS = ROW_TILE * TOP_K
RUN_BITS = ROW_TILE.bit_length()
ROW_SHAPE = (8, LANES)


def _row_copy(src, dst, sem):
    return pltpu.make_async_copy(src, dst, sem)


def _for_each_run_piece(cnt_ref, goff_ref, tile, fn):
    def body(e, lo):
        c = cnt_ref[tile * N_EXPERTS + e]
        g = goff_ref[tile * N_EXPERTS + e]
        for b in range(RUN_BITS):
            size = 1 << b

            @pl.when(jnp.bitwise_and(lax.shift_right_logical(c, b), 1) == 1)
            def _():
                off = jnp.bitwise_and(c, size - 1)
                fn(lo + off, g + off, size)
        return lo + c

    lax.fori_loop(0, N_EXPERTS, body, 0)


def _slot_onehot(lp, n_cols):
    pos = lax.broadcasted_iota(jnp.int32, (lp.shape[0], n_cols), 1)
    hit = pos == lp[:, 0:1]
    for k in range(1, TOP_K):
        hit = jnp.logical_or(hit, pos == lp[:, k:k + 1])
    return hit


def _dispatch_kernel(bv_ref, cnt_ref, goff_ref, lp_ref, x_ref, xs_hbm, zero_ref, buf_ref, sem, zsem):
    n_blocks = xs_hbm.shape[0] // MOE_TILE
    i = pl.program_id(0)
    nt = pl.num_programs(0)
    slot = i % 2

    def wait_slot(s):
        _row_copy(buf_ref.at[s], xs_hbm.at[pl.ds(0, SLOT_ROWS)], sem.at[s]).wait()

    @pl.when(pl.program_id(0) == 0)
    def _():
        zero_ref[...] = jnp.zeros(zero_ref.shape, F32)

        def fill(wait):
            def body(j, carry):
                @pl.when(bv_ref[j] < MOE_TILE)
                def _():
                    rows = pl.ds(pl.multiple_of(j * MOE_TILE, MOE_TILE), MOE_TILE)
                    cp = _row_copy(zero_ref, xs_hbm.at[rows], zsem)
                    if wait:
                        cp.wait()
                    else:
                        cp.start()
                return carry
            return body

        lax.fori_loop(0, n_blocks, fill(False), 0)
        lax.fori_loop(0, n_blocks, fill(True), 0)

    @pl.when(i >= 2)
    def _():
        wait_slot(slot)

    onehot = _slot_onehot(lp_ref[...], SLOT_ROWS).astype(BF16)
    rows = lax.dot_general(onehot, x_ref[...], (((0,), (0,)), ((), ())), preferred_element_type=F32)
    buf_ref[slot] = rows.reshape((SLOT_ROWS,) + ROW_SHAPE)

    def copy_out(lo, g, size):
        _row_copy(buf_ref.at[slot, pl.ds(lo, size)], xs_hbm.at[pl.ds(g, size)], sem.at[slot]).start()

    _for_each_run_piece(cnt_ref, goff_ref, i, copy_out)

    @pl.when(i == nt - 1)
    def _():
        wait_slot(slot)

        @pl.when(nt >= 2)
        def _():
            wait_slot(1 - slot)


def _dispatch(block_valid, cnt_flat, goff_flat, lp, b_in, n_rows):
    n_tok, D = b_in.shape
    grid_spec = pltpu.PrefetchScalarGridSpec(
        num_scalar_prefetch=3,
        grid=(n_tok // ROW_TILE,),
        in_specs=[pl.BlockSpec((ROW_TILE, LANES), lambda i, *_: (i, 0)),
                  pl.BlockSpec((ROW_TILE, D), lambda i, *_: (i, 0))],
        out_specs=pl.BlockSpec(memory_space=pl.ANY),
        scratch_shapes=[pltpu.VMEM((MOE_TILE,) + ROW_SHAPE, F32), pltpu.VMEM((2, SLOT_ROWS) + ROW_SHAPE, F32),
                        pltpu.SemaphoreType.DMA((2,)), pltpu.SemaphoreType.DMA(())],
    )
    return pl.pallas_call(
        _dispatch_kernel,
        grid_spec=grid_spec,
        out_shape=jax.ShapeDtypeStruct((n_rows,) + ROW_SHAPE, F32),
        compiler_params=_params("arbitrary"),
        name="moe_dispatch",
    )(block_valid, cnt_flat, goff_flat, lp, b_in)


GU_GROUP = 2 * LANES


def _pair_split_matrix():
    r = np.arange(GU_GROUP)
    p = np.zeros((GU_GROUP, GU_GROUP), np.float32)
    p[r, np.where(r % 2 == 0, r // 2, LANES + r // 2)] = 1.0
    return jnp.asarray(p, BF16)


def _expert_kernel(be_ref, bv_ref, x_ref, wgu_ref, bgu_ref, wd_ref, bd_ref, split_ref, y_ref, wgu_s, wd_s):
    j = pl.program_id(0)
    valid = bv_ref[j]
    n_groups = wgu_s.shape[1] // GU_GROUP
    changed = jnp.logical_or(j == 0, be_ref[j] != be_ref[jnp.maximum(j - 1, 0)])

    @pl.when(changed)
    def _():
        split = split_ref[...]
        for g in range(n_groups):
            cols = slice(g * GU_GROUP, (g + 1) * GU_GROUP)
            wgu_s[:, cols] = jnp.dot(wgu_ref[:, cols].astype(BF16), split,
                                     preferred_element_type=F32).astype(BF16)
        wd_s[...] = wd_ref[...].astype(BF16)

    def ffn(n):
        x = x_ref[0:n].reshape(n, wd_s.shape[1])
        rowi = lax.broadcasted_iota(jnp.int32, x.shape, 0)
        x = jnp.where(rowi < valid, x, 0.0).astype(BF16)
        gu = jnp.dot(x, wgu_s[...], preferred_element_type=F32) + bgu_ref[...]
        acts = []
        for g in range(n_groups):
            gate = jnp.minimum(gu[:, g * GU_GROUP:g * GU_GROUP + LANES], SWIGLU_LIMIT)
            up = jnp.clip(gu[:, g * GU_GROUP + LANES:(g + 1) * GU_GROUP], -SWIGLU_LIMIT, SWIGLU_LIMIT)
            acts.append(((up + 1.0) * gate * _sigmoid(SWIGLU_ALPHA * gate)).astype(BF16))
        act = jnp.concatenate(acts, axis=1)
        y = jnp.dot(act, wd_s[...], preferred_element_type=F32) + bd_ref[...]
        y_ref[0:n] = y.reshape((n,) + ROW_SHAPE)
        if n < MOE_TILE:
            y_ref[n:] = jnp.zeros((MOE_TILE - n,) + ROW_SHAPE, F32)

    half = MOE_TILE // 2

    @pl.when(valid > half)
    def _():
        ffn(MOE_TILE)

    @pl.when(jnp.logical_and(valid > 0, valid <= half))
    def _():
        ffn(half)

    @pl.when(valid <= 0)
    def _():
        y_ref[...] = jnp.zeros(y_ref.shape, F32)


def _experts(block_e, block_valid, xs, layer, w_gu, b_gu_split, w_down, b_down):
    P = xs.shape[0]
    _, E, D, W = w_gu.shape
    DE = w_down.shape[2]
    wmap = lambda j, be, bv: (layer, be[j], 0, 0)
    grid_spec = pltpu.PrefetchScalarGridSpec(
        num_scalar_prefetch=2,
        grid=(P // MOE_TILE,),
        in_specs=[pl.BlockSpec((MOE_TILE,) + ROW_SHAPE, lambda j, be, bv: (j, 0, 0)),
                  pl.BlockSpec((None, None, D, W), wmap),
                  pl.BlockSpec((None, None, 1, W), wmap),
                  pl.BlockSpec((None, None, DE, D), wmap),
                  pl.BlockSpec((None, None, 1, D), wmap),
                  pl.BlockSpec((GU_GROUP, GU_GROUP), lambda j, be, bv: (0, 0))],
        out_specs=pl.BlockSpec((MOE_TILE,) + ROW_SHAPE, lambda j, be, bv: (j, 0, 0)),
        scratch_shapes=[pltpu.VMEM((D, W), BF16), pltpu.VMEM((DE, D), BF16)],
    )
    return pl.pallas_call(
        _expert_kernel,
        grid_spec=grid_spec,
        out_shape=jax.ShapeDtypeStruct((P,) + ROW_SHAPE, F32),
        compiler_params=_params("arbitrary"),
        name="moe_experts",
    )(block_e, block_valid, xs, w_gu, b_gu_split, w_down, b_down, _pair_split_matrix())


def _combine_kernel(cnt_ref, goff_ref, ys_hbm, lp_ref, tw_ref, h_ref, mod_ref, lg_ref, lb_ref, o_ref,
                    buf_ref, sem):
    i = pl.program_id(0)
    nt = pl.num_programs(0)
    slot = i % 2

    def fetch(tile, s):
        def copy_in(lo, g, size):
            _row_copy(ys_hbm.at[pl.ds(g, size)], buf_ref.at[s, pl.ds(lo, size)], sem.at[s]).start()

        _for_each_run_piece(cnt_ref, goff_ref, tile, copy_in)

    @pl.when(i == 0)
    def _():
        fetch(0, 0)

    @pl.when(i + 1 < nt)
    def _():
        fetch(i + 1, 1 - slot)

    _row_copy(ys_hbm.at[pl.ds(0, SLOT_ROWS)], buf_ref.at[slot], sem.at[slot]).wait()

    lp = lp_ref[...]
    tw = tw_ref[...]
    pos = lax.broadcasted_iota(jnp.int32, (ROW_TILE, SLOT_ROWS), 1)
    wsel = jnp.where(pos == lp[:, 0:1], tw[:, 0:1], 0.0)
    for k in range(1, TOP_K):
        wsel = wsel + jnp.where(pos == lp[:, k:k + 1], tw[:, k:k + 1], 0.0)
    w_hi = wsel.astype(BF16)
    w_lo = (wsel - w_hi.astype(F32)).astype(BF16)
    ys = buf_ref[slot].reshape(SLOT_ROWS, h_ref.shape[1]).astype(BF16)
    f = (jnp.dot(w_hi, ys, preferred_element_type=F32) + jnp.dot(w_lo, ys, preferred_element_type=F32))
    m = mod_ref[...]
    o_ref[...] = _layernorm(DEEPNORM_ALPHA * h_ref[...] + m[5:6] * f, lg_ref[...], lb_ref[...])


def _combine(cnt_flat, goff_flat, ys, lp, tw, h1, mods, mod_map, ln_g, ln_b):
    n_tok, D = h1.shape
    const = lambda i, *_: (0, 0)
    row = lambda i, *_: (i, 0)
    grid_spec = pltpu.PrefetchScalarGridSpec(
        num_scalar_prefetch=2,
        grid=(n_tok // ROW_TILE,),
        in_specs=[pl.BlockSpec(memory_space=pl.ANY),
                  pl.BlockSpec((ROW_TILE, LANES), row),
                  pl.BlockSpec((ROW_TILE, LANES), row),
                  pl.BlockSpec((ROW_TILE, D), row),
                  pl.BlockSpec((None, 6, D), lambda i, *_: mod_map(i)),
                  pl.BlockSpec((1, D), const),
                  pl.BlockSpec((1, D), const)],
        out_specs=pl.BlockSpec((ROW_TILE, D), row),
        scratch_shapes=[pltpu.VMEM((2, SLOT_ROWS) + ROW_SHAPE, F32), pltpu.SemaphoreType.DMA((2,))],
    )
    return pl.pallas_call(
        _combine_kernel,
        grid_spec=grid_spec,
        out_shape=jax.ShapeDtypeStruct((n_tok, D), F32),
        compiler_params=_params("arbitrary"),
        name="moe_combine",
    )(cnt_flat, goff_flat, ys, lp, tw, h1, mods, ln_g, ln_b)


def _moe(b_in, plan, h1, mods, mod_map, ln_g, ln_b, layer, w_gu, b_gu, w_down, b_down):
    lp, tw, counts, run_t, carry_t = plan
    n_tok = b_in.shape[0]
    cnt = counts[0, :N_EXPERTS].astype(jnp.int32)
    pcnt = (cnt + MOE_TILE - 1) // MOE_TILE * MOE_TILE
    pends = jnp.cumsum(pcnt)
    pstart = pends - pcnt
    n_blocks = -(-(n_tok * TOP_K + N_EXPERTS * (MOE_TILE - 1)) // MOE_TILE)
    blk0 = jnp.arange(n_blocks, dtype=jnp.int32) * MOE_TILE
    block_e = jnp.minimum(jnp.sum((pends[None, :] <= blk0[:, None]).astype(jnp.int32), axis=1), N_EXPERTS - 1)
    block_valid = jnp.clip(cnt[block_e] - (blk0 - pstart[block_e]), 0, MOE_TILE).astype(jnp.int32)
    cnt_flat = run_t[:, 0, :N_EXPERTS].reshape(-1)
    goff_flat = (carry_t[:, 0, :N_EXPERTS] + pstart[None, :]).reshape(-1)
    xs = _dispatch(block_valid, cnt_flat, goff_flat, lp, b_in, n_blocks * MOE_TILE)
    L, E, W = b_gu.shape
    b_split = b_gu.reshape(L, E, W // GU_GROUP, LANES, 2).transpose(0, 1, 2, 4, 3).reshape(L, E, 1, W)
    ys = _experts(block_e, block_valid, xs, layer, w_gu, b_split, w_down, b_down.reshape(L, E, 1, -1))
    return _combine(cnt_flat, goff_flat, ys, lp, tw, h1, mods, mod_map, ln_g, ln_b)


def _router_weights(router_w, router_b):
    D = router_w.shape[0]
    rw = jnp.zeros((D, LANES), F32).at[:, :N_EXPERTS].set(router_w).astype(BF16)
    rb = jnp.full((1, LANES), -jnp.inf, F32).at[0, :N_EXPERTS].set(router_b)
    return rw, rb


def _glu_kernel(h_ref, mod_ref, w_ref, b_ref, u_out):
    m = mod_ref[...]
    a = (h_ref[...] * (1.0 + m[1:2]) + m[0:1]).astype(BF16)
    z = jnp.dot(a, w_ref[...], preferred_element_type=F32) + b_ref[...]
    half = z.shape[1] // 2
    u_out[...] = z[:, :half] * _sigmoid(z[:, half:])


def _glu(h3, mods, w, b, B, tpb_lat):
    D = D_MODEL
    return pl.pallas_call(
        _glu_kernel,
        grid=(B, tpb_lat),
        in_specs=[pl.BlockSpec((None, ROW_TILE, D), lambda b, j: (b, j + 1, 0)),
                  pl.BlockSpec((None, 6, D), lambda b, j: (b, 0, 0)),
                  pl.BlockSpec(w.shape, lambda b, j: (0, 0)),
                  pl.BlockSpec((1, w.shape[1]), lambda b, j: (0, 0))],
        out_specs=pl.BlockSpec((None, ROW_TILE, D), lambda b, j: (b, j, 0)),
        out_shape=jax.ShapeDtypeStruct((B, tpb_lat * ROW_TILE, D), F32),
        compiler_params=_params("arbitrary", "arbitrary"),
        name="conv_glu",
    )(h3, mods, w, b)


def _dwconv_kernel(prev_ref, cur_ref, next_ref, w_ref, b_ref, lg_ref, lb_ref, o_ref, win_ref, shift_ref, *,
                   tpb_lat):
    j = pl.program_id(1)
    H = CONV_HALO
    zero = jnp.zeros((H, cur_ref.shape[1]), F32)
    win_ref[0:H, :] = jnp.where(j > 0, prev_ref[...], zero)
    win_ref[H:H + ROW_TILE, :] = cur_ref[...]
    win_ref[H + ROW_TILE:, :] = jnp.where(j < tpb_lat - 1, next_ref[...], zero)
    w = w_ref[...]
    off = H - CONV_WIDTH // 2
    acc = jnp.zeros(cur_ref.shape, F32) + b_ref[...]
    sub = ROW_SHAPE[0]
    for r in range(sub):
        taps = [k for k in range(CONV_WIDTH) if (off + k) % sub == r]
        if not taps:
            continue
        src = win_ref
        if r:
            span = (off + taps[-1]) // sub * sub + ROW_TILE
            shift_ref[r % 2, 0:span, :] = win_ref[r:r + span, :]
            src = shift_ref.at[r % 2]
        for k in taps:
            a0 = (off + k) // sub * sub
            acc = acc + src[a0:a0 + ROW_TILE, :] * w[k:k + 1, :]
    y = _layernorm(acc, lg_ref[...], lb_ref[...])
    o_ref[...] = (y * _sigmoid(y)).astype(o_ref.dtype)


def _dwconv(u, w, b, ln_g, ln_b, B, tpb_lat):
    D = D_MODEL
    n = tpb_lat * ROW_TILE
    per = ROW_TILE // CONV_HALO
    last = n // CONV_HALO - 1
    const = lambda b_, j: (0, 0)
    return pl.pallas_call(
        functools.partial(_dwconv_kernel, tpb_lat=tpb_lat),
        grid=(B, tpb_lat),
        in_specs=[pl.BlockSpec((None, CONV_HALO, D), lambda b_, j: (b_, jnp.maximum(j * per - 1, 0), 0)),
                  pl.BlockSpec((None, ROW_TILE, D), lambda b_, j: (b_, j, 0)),
                  pl.BlockSpec((None, CONV_HALO, D), lambda b_, j: (b_, jnp.minimum((j + 1) * per, last), 0)),
                  pl.BlockSpec(w.shape, const),
                  pl.BlockSpec((1, D), const),
                  pl.BlockSpec((1, D), const),
                  pl.BlockSpec((1, D), const)],
        out_specs=pl.BlockSpec((ROW_TILE, D), lambda b_, j: (b_ * tpb_lat + j, 0)),
        out_shape=jax.ShapeDtypeStruct((B * n, D), BF16),
        scratch_shapes=[pltpu.VMEM((ROW_TILE + 2 * CONV_HALO, D), F32),
                        pltpu.VMEM((2, ROW_TILE + 2 * CONV_HALO, D), F32)],
        compiler_params=_params("arbitrary", "arbitrary"),
        name="conv_dw",
    )(u, u, u, w, b, ln_g, ln_b)


def kernel(x, c, ctx, c_ctx, mod_w, mod_b, ln1_g, ln1_b, ln2_g, ln2_b, mix_w_in, mla_q_norm, mla_kv_norm, mla_w_uq, mla_w_uk, mla_w_uv, hg_lb_logits, hg_norm, mix_w_out, conv_pw1_w, conv_pw1_b, conv_dw_w, conv_dw_b, conv_ln_g, conv_ln_b, conv_pw2_w, conv_pw2_b, router_w, router_b, exp_w_gu, exp_b_gu, exp_w_down, exp_b_down):
    B, n, D = x.shape
    tc = ctx.shape[1]
    assert D == D_MODEL and tc == ROW_TILE and n % ROW_TILE == 0 and n % GRID_W == 0
    assert B + 1 <= 16
    T = tc + n
    tpb = T // ROW_TILE
    tpb_lat = n // ROW_TILE
    N = B * T
    row2 = lambda v: v.reshape(1, -1)

    cond = jnp.zeros((16, D), F32).at[:B].set(c).at[B].set(c_ctx)
    mods = _modulation(cond, mod_w, mod_b)[:, :B + 1].reshape(DEPTH, B + 1, 6, D)
    lb_all = jnp.cumsum(jax.nn.softmax(hg_lb_logits.astype(F32), axis=0), axis=0)

    win, wq2, wkv = _inproj_weights(mix_w_in[0], mla_w_uq[0], mla_w_uk[0], mla_w_uv[0])
    cos_t, sin_t = _rope_tables(n, tc)
    q, k, v, hg = _inproj(ctx, x, mods[0], win, row2(mla_q_norm[0]), row2(mla_kv_norm[0]),
                          wq2, wkv, cos_t, sin_t, B, tpb)
    att = _attention(q, k, v, B, tpb, tc)
    hgo = _hgrn(hg, lb_all[0].reshape(HG_HEADS, 1, HG_DK), row2(hg_norm[0]), B, tpb)
    w_out = mix_w_out[0].astype(BF16)
    n_att = MLA_HEADS * MLA_V
    rw, rb = _router_weights(router_w[0], router_b[0])
    mod_map0 = lambda i: (jnp.where(i % tpb == 0, B, i // tpb), 0, 0)
    h1, b_in, *plan = _post(
        [att, hgo], [w_out[:n_att], w_out[n_att:]], jnp.zeros((1, D), F32),
        [ctx, x], [_ctx_map(tpb), _lat_map(tpb)], tpb, mods[0], mod_map0,
        row2(ln1_g[0]), row2(ln1_b[0]), rw, rb, N)
    ew = (exp_w_gu, exp_b_gu, exp_w_down, exp_b_down)
    h2 = _moe(b_in, plan, h1, mods[0], mod_map0, row2(ln2_g[0]), row2(ln2_b[0]), 0, *ew)

    n_lat = B * n
    h2_3 = h2.reshape(B * tpb, ROW_TILE, D)
    u = _glu(h2.reshape(B, T, D), mods[1], conv_pw1_w[0].astype(BF16), row2(conv_pw1_b[0]), B, tpb_lat)
    cv = _dwconv(u, conv_dw_w[0], row2(conv_dw_b[0]), row2(conv_ln_g[0]), row2(conv_ln_b[0]), B, tpb_lat)
    rw, rb = _router_weights(router_w[1], router_b[1])
    mod_map1 = lambda i: (i // tpb_lat, 0, 0)
    h_map1 = lambda i: ((i // tpb_lat) * tpb + i % tpb_lat + 1, 0, 0)
    h3, b_in, *plan = _post(
        [cv], [conv_pw2_w[0].astype(BF16)], row2(conv_pw2_b[0]),
        [h2_3], [h_map1], None, mods[1], mod_map1, row2(ln1_g[1]), row2(ln1_b[1]), rw, rb, n_lat)
    h4 = _moe(b_in, plan, h3, mods[1], mod_map1, row2(ln2_g[1]), row2(ln2_b[1]), 1, *ew)
    return h4.reshape(B, n, D)
```

```python
import functools

import numpy as np
import jax
import jax.numpy as jnp
from jax import lax
from jax.experimental import pallas as pl
from jax.experimental.pallas import tpu as pltpu

F32 = jnp.float32
BF16 = jnp.bfloat16

D_MODEL = 1024
DEPTH = 2
GRID_W = 64
MLA_HEADS = 8
MLA_Q_LORA = 384
MLA_KV_LORA = 256
MLA_NOPE = 64
MLA_ROPE = 32
MLA_V = 64
MLA_SCALE = (MLA_NOPE + MLA_ROPE) ** -0.5
LOG2_E = 1.4426950408889634
ROPE_BASE = 10000.0
HG_HEADS = 4
HG_DK = 128
HG_CHUNK = 32
HG_K = HG_HEADS * HG_DK
CONV_WIDTH = 31
N_EXPERTS = 32
TOP_K = 4
SWIGLU_LIMIT = 7.0
SWIGLU_ALPHA = 1.702
LN_EPS = 1e-5
RMS_EPS = 1e-6
DEEPNORM_ALPHA = (2.0 * DEPTH) ** 0.25

LANES = 128
ROW_TILE = 256
HEAD_BLOCK = 128
KEY_CHUNK = 1024
MOE_TILE = 512
CONV_HALO = 16
IN_WIDTH = 3456
HG_PARTS = 5 * HG_K
VMEM_LIMIT = 56 * 1024 * 1024


def _params(*sem):
    return pltpu.CompilerParams(dimension_semantics=sem, vmem_limit_bytes=VMEM_LIMIT)


def _sigmoid(x):
    return 1.0 / (1.0 + jnp.exp(-x))


def _layernorm(x, g, b):
    xc = x - jnp.mean(x, axis=-1, keepdims=True)
    var = jnp.mean(xc * xc, axis=-1, keepdims=True)
    return xc * lax.rsqrt(var + LN_EPS) * g + b


def _rms(x, g):
    return x * lax.rsqrt(jnp.mean(x * x, axis=-1, keepdims=True) + RMS_EPS) * g


def _mod_kernel(c_ref, w_ref, b_ref, o_ref):
    c = c_ref[...]
    s = (c * _sigmoid(c)).astype(BF16)
    o_ref[...] = jnp.dot(s, w_ref[...].astype(BF16), preferred_element_type=F32) + b_ref[...]


def _modulation(cond, mod_w, mod_b):
    L, D, W = mod_w.shape
    R = cond.shape[0]
    return pl.pallas_call(
        _mod_kernel,
        grid=(L, W // D),
        in_specs=[pl.BlockSpec((R, D), lambda l, n: (0, 0)),
                  pl.BlockSpec((None, D, D), lambda l, n: (l, 0, n)),
                  pl.BlockSpec((None, 1, D), lambda l, n: (l, 0, n))],
        out_specs=pl.BlockSpec((None, R, D), lambda l, n: (l, 0, n)),
        out_shape=jax.ShapeDtypeStruct((L, R, W), F32),
        compiler_params=_params("arbitrary", "arbitrary"),
        name="adaln_mod",
    )(cond, mod_w, mod_b.reshape(L, 1, W))


def _inproj_kernel(ctx_ref, x_ref, mod_ref, win_ref, qn_ref, kvn_ref, wq_ref, wkv_ref, c_ref, s_ref,
                   q_out, k_out, v_out, hg_out, *, tpb):
    m = mod_ref[...]
    h = jnp.where(pl.program_id(0) % tpb == 0, ctx_ref[...], x_ref[...])
    a = (h * (1.0 + m[1:2]) + m[0:1]).astype(BF16)
    z = jnp.dot(a, win_ref[...], preferred_element_type=F32)
    cos = c_ref[...]
    sin = s_ref[...]
    cos_h = jnp.concatenate([cos] * MLA_HEADS, axis=1)
    sin_h = jnp.concatenate([sin] * MLA_HEADS, axis=1)
    width = MLA_HEADS * HEAD_BLOCK

    cqn = _rms(z[:, :MLA_Q_LORA], qn_ref[...]).astype(BF16)
    qq = jnp.dot(cqn, wq_ref[...], preferred_element_type=F32)
    q_out[...] = ((qq[:, :width] * cos_h + qq[:, width:] * sin_h) * (MLA_SCALE * LOG2_E)).astype(BF16)

    c0 = MLA_Q_LORA + MLA_KV_LORA
    ckvn = _rms(z[:, MLA_Q_LORA:c0], kvn_ref[...]).astype(BF16)
    kv = jnp.dot(ckvn, wkv_ref[...], preferred_element_type=F32)
    r0 = c0 + HG_PARTS
    kr = z[:, r0:r0 + HEAD_BLOCK] * cos + z[:, r0 + HEAD_BLOCK:r0 + 2 * HEAD_BLOCK] * sin
    k_out[...] = (kv[:, :width] + jnp.concatenate([kr] * MLA_HEADS, axis=1)).astype(BF16)
    v_out[...] = kv[:, width:].astype(BF16)
    hg_out[...] = z[:, c0:r0]


def _rope_partner(w):
    shp = w.shape
    wr = w.reshape(shp[:-1] + (shp[-1] // 2, 2))
    return jnp.stack([-wr[..., 1], wr[..., 0]], axis=-1).reshape(shp)


def _inproj_weights(w_in, w_uq, w_uk, w_uv):
    D = w_in.shape[0]
    o_kr = MLA_Q_LORA + MLA_KV_LORA
    w_kr = w_in[:, o_kr:o_kr + MLA_ROPE]
    z_nope = jnp.zeros((D, MLA_NOPE), F32)
    z_pad = jnp.zeros((D, HEAD_BLOCK - MLA_NOPE - MLA_ROPE), F32)
    krb = jnp.concatenate([z_nope, w_kr, z_pad], axis=1)
    krb_p = jnp.concatenate([z_nope, _rope_partner(w_kr), z_pad], axis=1)
    win = jnp.concatenate([w_in[:, :o_kr], w_in[:, o_kr + MLA_ROPE:], krb, krb_p], axis=1).astype(BF16)

    R = w_uq.shape[0]
    wq = w_uq.reshape(R, MLA_HEADS, MLA_NOPE + MLA_ROPE)
    nope, rope = wq[..., :MLA_NOPE], wq[..., MLA_NOPE:]
    zp = jnp.zeros((R, MLA_HEADS, HEAD_BLOCK - MLA_NOPE - MLA_ROPE), F32)
    zn = jnp.zeros((R, MLA_HEADS, MLA_NOPE), F32)
    q_main = jnp.concatenate([nope, rope, zp], axis=-1).reshape(R, MLA_HEADS * HEAD_BLOCK)
    q_part = jnp.concatenate([zn, _rope_partner(rope), zp], axis=-1).reshape(R, MLA_HEADS * HEAD_BLOCK)
    wq2 = jnp.concatenate([q_main, q_part], axis=1).astype(BF16)

    C = w_uk.shape[0]
    wk = w_uk.reshape(C, MLA_HEADS, MLA_NOPE)
    wk = jnp.concatenate([wk, jnp.zeros((C, MLA_HEADS, HEAD_BLOCK - MLA_NOPE), F32)], axis=-1)
    wv = w_uv.reshape(C, MLA_HEADS // 2, 2, MLA_V)
    zv = jnp.zeros((C, MLA_HEADS // 2, MLA_V), F32)
    wv = jnp.stack([jnp.concatenate([wv[:, :, 0], zv], -1), jnp.concatenate([zv, wv[:, :, 1]], -1)], axis=2)
    wkv = jnp.concatenate([wk.reshape(C, -1), wv.reshape(C, -1)], axis=1).astype(BF16)
    return win, wq2, wkv


def _rope_tables(n, tc):
    rows = n // GRID_W
    pos_r = jnp.repeat(jnp.arange(rows, dtype=F32), GRID_W)
    pos_c = jnp.tile(jnp.arange(GRID_W, dtype=F32), rows)
    half = MLA_ROPE // 2
    inv = jnp.power(ROPE_BASE, -jnp.arange(0, half, 2, dtype=F32) / half)
    ang = jnp.concatenate([pos_r[:, None] * inv, pos_c[:, None] * inv], axis=-1)
    cos2 = jnp.repeat(jnp.cos(ang), 2, axis=-1)
    sin2 = jnp.repeat(jnp.sin(ang), 2, axis=-1)
    pad = HEAD_BLOCK - MLA_NOPE - MLA_ROPE
    cos_l = jnp.concatenate([jnp.ones((n, MLA_NOPE), F32), cos2, jnp.ones((n, pad), F32)], axis=1)
    sin_l = jnp.concatenate([jnp.zeros((n, MLA_NOPE), F32), sin2, jnp.zeros((n, pad), F32)], axis=1)
    cos_t = jnp.concatenate([jnp.ones((tc, HEAD_BLOCK), F32), cos_l], axis=0)
    sin_t = jnp.concatenate([jnp.zeros((tc, HEAD_BLOCK), F32), sin_l], axis=0)
    return cos_t, sin_t


def _ctx_map(tpb):
    return lambda i: (i // tpb, 0, 0)


def _lat_map(tpb):
    return lambda i: (i // tpb, jnp.maximum(i % tpb - 1, 0), 0)


def _inproj(ctx, x, mods, win, qn, kvn, wq2, wkv, cos_t, sin_t, B, tpb):
    D = x.shape[2]
    N = B * tpb * ROW_TILE
    ctx_row = mods.shape[0] - 1
    width = MLA_HEADS * HEAD_BLOCK
    const = lambda i: (0, 0)
    return pl.pallas_call(
        functools.partial(_inproj_kernel, tpb=tpb),
        grid=(N // ROW_TILE,),
        in_specs=[pl.BlockSpec((None, ROW_TILE, D), _ctx_map(tpb)),
                  pl.BlockSpec((None, ROW_TILE, D), _lat_map(tpb)),
                  pl.BlockSpec((None, 6, D), lambda i: (jnp.where(i % tpb == 0, ctx_row, i // tpb), 0, 0)),
                  pl.BlockSpec(win.shape, const),
                  pl.BlockSpec(qn.shape, const),
                  pl.BlockSpec(kvn.shape, const),
                  pl.BlockSpec(wq2.shape, const),
                  pl.BlockSpec(wkv.shape, const),
                  pl.BlockSpec((ROW_TILE, HEAD_BLOCK), lambda i: (i % tpb, 0)),
                  pl.BlockSpec((ROW_TILE, HEAD_BLOCK), lambda i: (i % tpb, 0))],
        out_specs=[pl.BlockSpec((ROW_TILE, width), lambda i: (i, 0)),
                   pl.BlockSpec((ROW_TILE, width), lambda i: (i, 0)),
                   pl.BlockSpec((ROW_TILE, width), lambda i: (i, 0)),
                   pl.BlockSpec((ROW_TILE, HG_PARTS), lambda i: (i, 0))],
        out_shape=[jax.ShapeDtypeStruct((N, width), BF16),
                   jax.ShapeDtypeStruct((N, width), BF16),
                   jax.ShapeDtypeStruct((N, width), BF16),
                   jax.ShapeDtypeStruct((N, HG_PARTS), F32)],
        compiler_params=_params("arbitrary"),
        name="inproj",
    )(ctx, x, mods, win, qn, kvn, wq2, wkv, cos_t, sin_t)


ATTN_HEADS = 4
SCORE_LOOKAHEAD = 1


def _attn_kernel(q_ref, k_ref, v_ref, o_ref, *, chunks):
    j = pl.program_id(2)

    def scores(hh, n_chunks):
        lo, hi = hh * HEAD_BLOCK, (hh + 1) * HEAD_BLOCK
        q = q_ref[:, lo:hi]
        ss, m = [], None
        for s0, s1 in chunks[:n_chunks]:
            s = lax.dot_general(q, k_ref[s0:s1, lo:hi], (((1,), (1,)), ((), ())), preferred_element_type=F32)
            mc = jnp.max(s, axis=-1, keepdims=True)
            m = mc if m is None else jnp.maximum(m, mc)
            ss.append(s)
        return ss, m

    def softmax_values(hh, scored, n_chunks):
        lo, hi = hh * HEAD_BLOCK, (hh + 1) * HEAD_BLOCK
        ss, m = scored
        l = acc = None
        for s, (s0, s1) in zip(ss, chunks[:n_chunks]):
            p = jnp.exp2(s - m)
            lc = jnp.sum(p, axis=-1, keepdims=True)
            ac = jnp.dot(p.astype(BF16), v_ref[s0:s1, lo:hi], preferred_element_type=F32)
            l = lc if l is None else l + lc
            acc = ac if acc is None else acc + ac
        return acc / l

    def pairs(n_chunks):
        heads = [None] * ATTN_HEADS
        ss = [scores(hh, n_chunks) for hh in range(min(SCORE_LOOKAHEAD, ATTN_HEADS))]
        for hh in range(ATTN_HEADS):
            if hh + SCORE_LOOKAHEAD < ATTN_HEADS:
                ss.append(scores(hh + SCORE_LOOKAHEAD, n_chunks))
            heads[hh] = softmax_values(hh, ss[hh], n_chunks)
            ss[hh] = None
        outs = [heads[hh] + heads[hh + 1] for hh in range(0, ATTN_HEADS, 2)]
        o_ref[...] = jnp.concatenate(outs, axis=1).astype(o_ref.dtype)

    @pl.when(j == 0)
    def _():
        pairs(1)

    @pl.when(j > 0)
    def _():
        pairs(len(chunks))


def _attention(q, k, v, B, tpb, tc):
    N = q.shape[0]
    T = tpb * ROW_TILE
    kc = min(KEY_CHUNK, T - tc)
    chunks = ((0, tc),) + tuple((s, s + kc) for s in range(tc, T, kc))
    width = ATTN_HEADS * HEAD_BLOCK
    return pl.pallas_call(
        functools.partial(_attn_kernel, chunks=chunks),
        grid=(B, MLA_HEADS // ATTN_HEADS, tpb),
        in_specs=[pl.BlockSpec((ROW_TILE, width), lambda b, p, j: (b * tpb + j, p)),
                  pl.BlockSpec((T, width), lambda b, p, j: (b, p)),
                  pl.BlockSpec((T, width), lambda b, p, j: (b, p))],
        out_specs=pl.BlockSpec((ROW_TILE, ATTN_HEADS * MLA_V), lambda b, p, j: (b * tpb + j, p)),
        out_shape=jax.ShapeDtypeStruct((N, MLA_HEADS * MLA_V), BF16),
        compiler_params=_params("arbitrary", "arbitrary", "arbitrary"),
        name="mla_attention",
    )(q, k, v)


def _hgrn_kernel(q_ref, ff_ref, fb_ref, v_ref, g_ref, lb_ref, gn_ref, o_ref, oacc_ref, obwd_ref, *, tpb):
    lb = lb_ref[...]
    gn = gn_ref[...]
    C = HG_CHUNK
    per_tile = ROW_TILE // C
    row = lax.broadcasted_iota(jnp.int32, (ROW_TILE, ROW_TILE), 0)
    col = lax.broadcasted_iota(jnp.int32, (ROW_TILE, ROW_TILE), 1)
    same = (row // C) == (col // C)
    ones_blk = same.astype(BF16)
    row_chunk = lax.broadcasted_iota(jnp.int32, (ROW_TILE, HG_DK), 0) // C
    nt_dims = (((1,), (1,)), ((), ()))
    tn_dims = (((0,), (0,)), ((), ()))

    def tile_rows(t):
        return pl.ds(pl.multiple_of(t * ROW_TILE, ROW_TILE), ROW_TILE)

    def chunk_sums(m01, g3):
        return sum(jnp.dot(m01, part, preferred_element_type=F32) for part in g3)

    def per_chunk_blocks(x):
        zero = jnp.zeros(x.shape, x.dtype)
        return jnp.concatenate([jnp.where(row_chunk == c, x, zero) for c in range(per_tile)], axis=1)

    def tile_step(t, st, f_ref, reverse, acc_ref):
        rows = tile_rows(t)
        f = lb + (1.0 - lb) * _sigmoid(f_ref[rows, :])
        kk = 1.0 - f
        g = jnp.log(f)
        g_hi = g.astype(BF16)
        r1 = g - g_hi.astype(F32)
        g_mid = r1.astype(BF16)
        g3 = (g_hi, g_mid, (r1 - g_mid.astype(F32)).astype(BF16))
        in_chunk = same & ((col >= row) if reverse else (col <= row))
        bcum = chunk_sums(in_chunk.astype(BF16), g3)
        btot = chunk_sums(ones_blk, g3)
        yield
        qin = (q_ref[rows, :] * jnp.exp(bcum)).astype(BF16)
        kdec = (kk * jnp.exp(-bcum)).astype(BF16)
        krem = (kk * jnp.exp(btot - bcum)).astype(BF16)
        dec = jnp.exp(btot)
        v = v_ref[rows, :].astype(BF16)
        a = lax.dot_general(qin, kdec, nt_dims, preferred_element_type=F32)
        ut = lax.dot_general(v, per_chunk_blocks(krem), tn_dims, preferred_element_type=F32)
        yield
        a = jnp.where(in_chunk, a, 0.0).astype(BF16)
        o_t = jnp.dot(a, v, preferred_element_type=F32)
        states = [None] * per_tile
        for c in (range(per_tile - 1, -1, -1) if reverse else range(per_tile)):
            states[c] = st.astype(BF16)
            st = st * dec[c * C:c * C + 1, :] + ut[:, c * HG_DK:(c + 1) * HG_DK]
        yield
        o_t = o_t + lax.dot_general(per_chunk_blocks(qin), jnp.concatenate(states, axis=1), nt_dims,
                                    preferred_element_type=F32)
        acc_ref[rows, :] = o_t
        return st

    def both(i, sts):
        gens = [tile_step(i, sts[0], ff_ref, False, oacc_ref),
                tile_step(jnp.where(i == 0, 0, tpb - i), sts[1], fb_ref, True, obwd_ref)]
        out = [None, None]
        while any(o is None for o in out):
            for n, gen in enumerate(gens):
                if out[n] is None:
                    try:
                        next(gen)
                    except StopIteration as done:
                        out[n] = done.value
        return tuple(out)

    s0 = jnp.zeros((HG_DK, HG_DK), F32)
    lax.fori_loop(0, tpb, both, (s0, s0))

    def readout(t, carry):
        rows = tile_rows(t)
        gate = g_ref[rows, :]
        y = _rms(oacc_ref[rows, :] + obwd_ref[rows, :], gn) * (gate * _sigmoid(gate))
        o_ref[rows, :] = y.astype(o_ref.dtype)
        return carry

    lax.fori_loop(0, tpb, readout, 0)


def _hgrn(hg, lb, gn, B, tpb):
    N = hg.shape[0]
    T = tpb * ROW_TILE
    part = lambda p: pl.BlockSpec((T, HG_DK), lambda b, h: (b, p * HG_HEADS + h))
    return pl.pallas_call(
        functools.partial(_hgrn_kernel, tpb=tpb),
        grid=(B, HG_HEADS),
        in_specs=[part(0), part(1), part(2), part(3), part(4),
                  pl.BlockSpec((None, 1, HG_DK), lambda b, h: (h, 0, 0)),
                  pl.BlockSpec((1, HG_DK), lambda b, h: (0, 0))],
        out_specs=pl.BlockSpec((T, HG_DK), lambda b, h: (b, h)),
        out_shape=jax.ShapeDtypeStruct((N, HG_K), BF16),
        scratch_shapes=[pltpu.VMEM((T, HG_DK), F32), pltpu.VMEM((T, HG_DK), F32)],
        compiler_params=_params("arbitrary", "arbitrary"),
        name="hgrn2",
    )(hg, hg, hg, hg, hg, lb, gn)


def _post_kernel(*refs, n_in, ctx_period):
    xs = refs[:n_in]
    ws = refs[n_in:2 * n_in]
    n_res = 1 if ctx_period is None else 2
    h_refs = refs[2 * n_in + 1:2 * n_in + 1 + n_res]
    bias_ref = refs[2 * n_in]
    (mod_ref, lg_ref, lb_ref, rw_ref, rb_ref,
     h1_out, bin_out, lp_out, tw_out, cnt_out, run_out, carry_out) = refs[2 * n_in + 1 + n_res:]
    i = pl.program_id(0)
    y = bias_ref[...]
    for x_ref, w_ref in zip(xs, ws):
        y = y + jnp.dot(x_ref[...], w_ref[...], preferred_element_type=F32)
    m = mod_ref[...]
    if ctx_period is None:
        h = h_refs[0][...]
    else:
        h = jnp.where(i % ctx_period == 0, h_refs[0][...], h_refs[1][...])
    h1 = _layernorm(DEEPNORM_ALPHA * h + m[2:3] * y, lg_ref[...], lb_ref[...])
    h1_out[...] = h1
    b_in = (h1 * (1.0 + m[4:5]) + m[3:4]).astype(BF16)
    bin_out[...] = b_in
    lg = jnp.dot(b_in, rw_ref[...], preferred_element_type=F32) + rb_ref[...]
    lane = lax.broadcasted_iota(jnp.int32, lg.shape, 1)
    vals, hits = [], []
    member = jnp.zeros(lg.shape, F32)
    for _ in range(TOP_K):
        mx = jnp.max(lg, axis=-1, keepdims=True)
        ix = jnp.min(jnp.where(lg == mx, lane, LANES), axis=-1, keepdims=True)
        hit = lane == ix
        member = member + hit.astype(F32)
        lg = jnp.where(hit, -jnp.inf, lg)
        vals.append(mx)
        hits.append(hit)
    es = [jnp.exp(v - vals[0]) for v in vals]
    den = es[0] + es[1] + es[2] + es[3]

    row = lax.broadcasted_iota(jnp.int32, (ROW_TILE, ROW_TILE), 0)
    col = lax.broadcasted_iota(jnp.int32, (ROW_TILE, ROW_TILE), 1)
    rank = jnp.dot((col < row).astype(BF16), member.astype(BF16), preferred_element_type=F32)
    cnt = jnp.sum(member, axis=0, keepdims=True)
    er = lax.broadcasted_iota(jnp.int32, (LANES, LANES), 0)
    ec = lax.broadcasted_iota(jnp.int32, (LANES, LANES), 1)
    cnt8 = jnp.broadcast_to(cnt, (8, LANES))
    loff = jnp.dot(cnt8.astype(BF16), (er < ec).astype(BF16), preferred_element_type=F32)[0:1, :]
    base = rank + loff
    lp = jnp.zeros(lg.shape, F32)
    tw = jnp.zeros(lg.shape, F32)
    for k in range(TOP_K):
        pk = jnp.sum(jnp.where(hits[k], base, 0.0), axis=-1, keepdims=True)
        lp = jnp.where(lane == k, pk, lp)
        tw = jnp.where(lane == k, es[k] / den, tw)
    lp_out[...] = lp.astype(jnp.int32)
    tw_out[...] = tw
    run_out[...] = cnt8.astype(jnp.int32)

    @pl.when(i == 0)
    def _():
        cnt_out[...] = jnp.zeros(cnt_out.shape, F32)

    carry_out[...] = jnp.broadcast_to(cnt_out[0:1, :], (8, LANES)).astype(jnp.int32)
    cnt_out[0:1, :] += cnt


def _post(xs, ws, bias, hs, h_maps, ctx_period, mods, mod_map, ln_g, ln_b, rw, rb, n_tok):
    D = D_MODEL
    n_in = len(xs)
    const = lambda i: (0, 0)
    row = lambda i: (i, 0)
    in_specs = ([pl.BlockSpec((ROW_TILE, x.shape[1]), row) for x in xs]
                + [pl.BlockSpec(w.shape, const) for w in ws]
                + [pl.BlockSpec((1, D), const)]
                + [pl.BlockSpec((None, ROW_TILE, D), h_map) for h_map in h_maps]
                + [pl.BlockSpec((None, 6, D), mod_map),
                   pl.BlockSpec((1, D), const),
                   pl.BlockSpec((1, D), const),
                   pl.BlockSpec((D, LANES), const),
                   pl.BlockSpec((1, LANES), const)])
    nt = n_tok // ROW_TILE
    per_tile = pl.BlockSpec((None, 8, LANES), lambda i: (i, 0, 0))
    return pl.pallas_call(
        functools.partial(_post_kernel, n_in=n_in, ctx_period=ctx_period),
        grid=(nt,),
        in_specs=in_specs,
        out_specs=[pl.BlockSpec((ROW_TILE, D), row),
                   pl.BlockSpec((ROW_TILE, D), row),
                   pl.BlockSpec((ROW_TILE, LANES), row),
                   pl.BlockSpec((ROW_TILE, LANES), row),
                   pl.BlockSpec((8, LANES), const),
                   per_tile,
                   per_tile],
        out_shape=[jax.ShapeDtypeStruct((n_tok, D), F32),
                   jax.ShapeDtypeStruct((n_tok, D), BF16),
                   jax.ShapeDtypeStruct((n_tok, LANES), jnp.int32),
                   jax.ShapeDtypeStruct((n_tok, LANES), F32),
                   jax.ShapeDtypeStruct((8, LANES), F32),
                   jax.ShapeDtypeStruct((nt, 8, LANES), jnp.int32),
                   jax.ShapeDtypeStruct((nt, 8, LANES), jnp.int32)],
        compiler_params=_params("arbitrary"),
        name="post_mixer",
    )(*xs, *ws, bias, *hs, mods, ln_g, ln_b, rw, rb)


SLOT_ROWS = ROW_TILE * TOP_K
RUN_BITS = ROW_TILE.bit_length()
ROW_SHAPE = (8, LANES)


def _row_copy(src, dst, sem):
    return pltpu.make_async_copy(src, dst, sem)


def _for_each_run_piece(cnt_ref, goff_ref, tile, fn):
    def body(e, lo):
        c = cnt_ref[tile * N_EXPERTS + e]
        g = goff_ref[tile * N_EXPERTS + e]
        for b in range(RUN_BITS):
            size = 1 << b

            @pl.when(jnp.bitwise_and(lax.shift_right_logical(c, b), 1) == 1)
            def _():
                off = jnp.bitwise_and(c, size - 1)
                fn(lo + off, g + off, size)
        return lo + c

    lax.fori_loop(0, N_EXPERTS, body, 0)


def _slot_onehot(lp, n_cols):
    pos = lax.broadcasted_iota(jnp.int32, (lp.shape[0], n_cols), 1)
    hit = pos == lp[:, 0:1]
    for k in range(1, TOP_K):
        hit = jnp.logical_or(hit, pos == lp[:, k:k + 1])
    return hit


def _dispatch_kernel(bv_ref, cnt_ref, goff_ref, lp_ref, x_ref, xs_hbm, zero_ref, buf_ref, sem, zsem):
    n_blocks = xs_hbm.shape[0] // MOE_TILE
    i = pl.program_id(0)
    nt = pl.num_programs(0)
    slot = i % 2

    def wait_slot(s):
        _row_copy(buf_ref.at[s], xs_hbm.at[pl.ds(0, SLOT_ROWS)], sem.at[s]).wait()

    @pl.when(pl.program_id(0) == 0)
    def _():
        zero_ref[...] = jnp.zeros(zero_ref.shape, F32)

        def fill(wait):
            def body(j, carry):
                @pl.when(bv_ref[j] < MOE_TILE)
                def _():
                    rows = pl.ds(pl.multiple_of(j * MOE_TILE, MOE_TILE), MOE_TILE)
                    cp = _row_copy(zero_ref, xs_hbm.at[rows], zsem)
                    if wait:
                        cp.wait()
                    else:
                        cp.start()
                return carry
            return body

        lax.fori_loop(0, n_blocks, fill(False), 0)
        lax.fori_loop(0, n_blocks, fill(True), 0)

    @pl.when(i >= 2)
    def _():
        wait_slot(slot)

    onehot = _slot_onehot(lp_ref[...], SLOT_ROWS).astype(BF16)
    rows = lax.dot_general(onehot, x_ref[...], (((0,), (0,)), ((), ())), preferred_element_type=F32)
    buf_ref[slot] = rows.reshape((SLOT_ROWS,) + ROW_SHAPE)

    def copy_out(lo, g, size):
        _row_copy(buf_ref.at[slot, pl.ds(lo, size)], xs_hbm.at[pl.ds(g, size)],
                  sem.at[slot]).start(priority=size.bit_length() % 2)

    _for_each_run_piece(cnt_ref, goff_ref, i, copy_out)

    @pl.when(i == nt - 1)
    def _():
        wait_slot(slot)

        @pl.when(nt >= 2)
        def _():
            wait_slot(1 - slot)


def _dispatch(block_valid, cnt_flat, goff_flat, lp, b_in, n_rows):
    n_tok, D = b_in.shape
    grid_spec = pltpu.PrefetchScalarGridSpec(
        num_scalar_prefetch=3,
        grid=(n_tok // ROW_TILE,),
        in_specs=[pl.BlockSpec((ROW_TILE, LANES), lambda i, *_: (i, 0)),
                  pl.BlockSpec((ROW_TILE, D), lambda i, *_: (i, 0))],
        out_specs=pl.BlockSpec(memory_space=pl.ANY),
        scratch_shapes=[pltpu.VMEM((MOE_TILE,) + ROW_SHAPE, F32), pltpu.VMEM((2, SLOT_ROWS) + ROW_SHAPE, F32),
                        pltpu.SemaphoreType.DMA((2,)), pltpu.SemaphoreType.DMA(())],
    )
    return pl.pallas_call(
        _dispatch_kernel,
        grid_spec=grid_spec,
        out_shape=jax.ShapeDtypeStruct((n_rows,) + ROW_SHAPE, F32),
        compiler_params=_params("arbitrary"),
        name="moe_dispatch",
    )(block_valid, cnt_flat, goff_flat, lp, b_in)


GU_GROUP = 2 * LANES


def _pair_split_matrix():
    r = np.arange(GU_GROUP)
    p = np.zeros((GU_GROUP, GU_GROUP), np.float32)
    p[r, np.where(r % 2 == 0, r // 2, LANES + r // 2)] = 1.0
    return jnp.asarray(p, BF16)


def _expert_kernel(be_ref, bv_ref, x_ref, wgu_ref, bgu_ref, wd_ref, bd_ref, split_ref, y_ref, wgu_s, wd_s):
    j = pl.program_id(0)
    valid = bv_ref[j]
    n_groups = wgu_s.shape[1] // GU_GROUP
    changed = jnp.logical_or(j == 0, be_ref[j] != be_ref[jnp.maximum(j - 1, 0)])

    @pl.when(changed)
    def _():
        split = split_ref[...]
        for g in range(n_groups):
            cols = slice(g * GU_GROUP, (g + 1) * GU_GROUP)
            wgu_s[:, cols] = jnp.dot(wgu_ref[:, cols].astype(BF16), split,
                                     preferred_element_type=F32).astype(BF16)
        wd_s[...] = wd_ref[...].astype(BF16)

    def ffn(n):
        x = x_ref[0:n].reshape(n, wd_s.shape[1])
        rowi = lax.broadcasted_iota(jnp.int32, x.shape, 0)
        x = jnp.where(rowi < valid, x, 0.0).astype(BF16)
        gu = jnp.dot(x, wgu_s[...], preferred_element_type=F32) + bgu_ref[...]
        acts = []
        for g in range(n_groups):
            gate = jnp.minimum(gu[:, g * GU_GROUP:g * GU_GROUP + LANES], SWIGLU_LIMIT)
            up = jnp.clip(gu[:, g * GU_GROUP + LANES:(g + 1) * GU_GROUP], -SWIGLU_LIMIT, SWIGLU_LIMIT)
            acts.append(((up + 1.0) * gate * _sigmoid(SWIGLU_ALPHA * gate)).astype(BF16))
        act = jnp.concatenate(acts, axis=1)
        y = jnp.dot(act, wd_s[...], preferred_element_type=F32) + bd_ref[...]
        y_ref[0:n] = y.reshape((n,) + ROW_SHAPE)
        if n < MOE_TILE:
            y_ref[n:] = jnp.zeros((MOE_TILE - n,) + ROW_SHAPE, F32)

    half = MOE_TILE // 2

    @pl.when(valid > half)
    def _():
        ffn(MOE_TILE)

    @pl.when(jnp.logical_and(valid > 0, valid <= half))
    def _():
        ffn(half)

    @pl.when(valid <= 0)
    def _():
        y_ref[...] = jnp.zeros(y_ref.shape, F32)


def _experts(block_e, block_valid, xs, layer, w_gu, b_gu_split, w_down, b_down):
    P = xs.shape[0]
    _, E, D, W = w_gu.shape
    DE = w_down.shape[2]
    wmap = lambda j, be, bv: (layer, be[j], 0, 0)
    grid_spec = pltpu.PrefetchScalarGridSpec(
        num_scalar_prefetch=2,
        grid=(P // MOE_TILE,),
        in_specs=[pl.BlockSpec((MOE_TILE,) + ROW_SHAPE, lambda j, be, bv: (j, 0, 0)),
                  pl.BlockSpec((None, None, D, W), wmap),
                  pl.BlockSpec((None, None, 1, W), wmap),
                  pl.BlockSpec((None, None, DE, D), wmap),
                  pl.BlockSpec((None, None, 1, D), wmap),
                  pl.BlockSpec((GU_GROUP, GU_GROUP), lambda j, be, bv: (0, 0))],
        out_specs=pl.BlockSpec((MOE_TILE,) + ROW_SHAPE, lambda j, be, bv: (j, 0, 0)),
        scratch_shapes=[pltpu.VMEM((D, W), BF16), pltpu.VMEM((DE, D), BF16)],
    )
    return pl.pallas_call(
        _expert_kernel,
        grid_spec=grid_spec,
        out_shape=jax.ShapeDtypeStruct((P,) + ROW_SHAPE, F32),
        compiler_params=_params("arbitrary"),
        name="moe_experts",
    )(block_e, block_valid, xs, w_gu, b_gu_split, w_down, b_down, _pair_split_matrix())


def _combine_kernel(cnt_ref, goff_ref, ys_hbm, lp_ref, tw_ref, h_ref, mod_ref, lg_ref, lb_ref, o_ref,
                    buf_ref, sem):
    i = pl.program_id(0)
    nt = pl.num_programs(0)
    slot = i % 2

    def fetch(tile, s):
        def copy_in(lo, g, size):
            _row_copy(ys_hbm.at[pl.ds(g, size)], buf_ref.at[s, pl.ds(lo, size)],
                      sem.at[s]).start(priority=size.bit_length() % 2)

        _for_each_run_piece(cnt_ref, goff_ref, tile, copy_in)

    @pl.when(i == 0)
    def _():
        fetch(0, 0)

    @pl.when(i + 1 < nt)
    def _():
        fetch(i + 1, 1 - slot)

    _row_copy(ys_hbm.at[pl.ds(0, SLOT_ROWS)], buf_ref.at[slot], sem.at[slot]).wait()

    lp = lp_ref[...]
    tw = tw_ref[...]
    pos = lax.broadcasted_iota(jnp.int32, (ROW_TILE, SLOT_ROWS), 1)
    wsel = jnp.where(pos == lp[:, 0:1], tw[:, 0:1], 0.0)
    for k in range(1, TOP_K):
        wsel = wsel + jnp.where(pos == lp[:, k:k + 1], tw[:, k:k + 1], 0.0)
    w_hi = wsel.astype(BF16)
    w_lo = (wsel - w_hi.astype(F32)).astype(BF16)
    ys = buf_ref[slot].reshape(SLOT_ROWS, h_ref.shape[1]).astype(BF16)
    f = (jnp.dot(w_hi, ys, preferred_element_type=F32) + jnp.dot(w_lo, ys, preferred_element_type=F32))
    m = mod_ref[...]
    o_ref[...] = _layernorm(DEEPNORM_ALPHA * h_ref[...] + m[5:6] * f, lg_ref[...], lb_ref[...])


def _combine(cnt_flat, goff_flat, ys, lp, tw, h1, mods, mod_map, ln_g, ln_b):
    n_tok, D = h1.shape
    const = lambda i, *_: (0, 0)
    row = lambda i, *_: (i, 0)
    grid_spec = pltpu.PrefetchScalarGridSpec(
        num_scalar_prefetch=2,
        grid=(n_tok // ROW_TILE,),
        in_specs=[pl.BlockSpec(memory_space=pl.ANY),
                  pl.BlockSpec((ROW_TILE, LANES), row),
                  pl.BlockSpec((ROW_TILE, LANES), row),
                  pl.BlockSpec((ROW_TILE, D), row),
                  pl.BlockSpec((None, 6, D), lambda i, *_: mod_map(i)),
                  pl.BlockSpec((1, D), const),
                  pl.BlockSpec((1, D), const)],
        out_specs=pl.BlockSpec((ROW_TILE, D), row),
        scratch_shapes=[pltpu.VMEM((2, SLOT_ROWS) + ROW_SHAPE, F32), pltpu.SemaphoreType.DMA((2,))],
    )
    return pl.pallas_call(
        _combine_kernel,
        grid_spec=grid_spec,
        out_shape=jax.ShapeDtypeStruct((n_tok, D), F32),
        compiler_params=_params("arbitrary"),
        name="moe_combine",
    )(cnt_flat, goff_flat, ys, lp, tw, h1, mods, ln_g, ln_b)


def _moe(b_in, plan, h1, mods, mod_map, ln_g, ln_b, layer, w_gu, b_gu, w_down, b_down):
    lp, tw, counts, run_t, carry_t = plan
    n_tok = b_in.shape[0]
    cnt = counts[0, :N_EXPERTS].astype(jnp.int32)
    pcnt = (cnt + MOE_TILE - 1) // MOE_TILE * MOE_TILE
    pends = jnp.cumsum(pcnt)
    pstart = pends - pcnt
    n_blocks = -(-(n_tok * TOP_K + N_EXPERTS * (MOE_TILE - 1)) // MOE_TILE)
    blk0 = jnp.arange(n_blocks, dtype=jnp.int32) * MOE_TILE
    block_e = jnp.minimum(jnp.sum((pends[None, :] <= blk0[:, None]).astype(jnp.int32), axis=1), N_EXPERTS - 1)
    block_valid = jnp.clip(cnt[block_e] - (blk0 - pstart[block_e]), 0, MOE_TILE).astype(jnp.int32)
    cnt_flat = run_t[:, 0, :N_EXPERTS].reshape(-1)
    goff_flat = (carry_t[:, 0, :N_EXPERTS] + pstart[None, :]).reshape(-1)
    xs = _dispatch(block_valid, cnt_flat, goff_flat, lp, b_in, n_blocks * MOE_TILE)
    L, E, W = b_gu.shape
    b_split = b_gu.reshape(L, E, W // GU_GROUP, LANES, 2).transpose(0, 1, 2, 4, 3).reshape(L, E, 1, W)
    ys = _experts(block_e, block_valid, xs, layer, w_gu, b_split, w_down, b_down.reshape(L, E, 1, -1))
    return _combine(cnt_flat, goff_flat, ys, lp, tw, h1, mods, mod_map, ln_g, ln_b)


def _router_weights(router_w, router_b):
    D = router_w.shape[0]
    rw = jnp.zeros((D, LANES), F32).at[:, :N_EXPERTS].set(router_w).astype(BF16)
    rb = jnp.full((1, LANES), -jnp.inf, F32).at[0, :N_EXPERTS].set(router_b)
    return rw, rb


def _glu_kernel(h_ref, mod_ref, w_ref, b_ref, u_out):
    m = mod_ref[...]
    a = (h_ref[...] * (1.0 + m[1:2]) + m[0:1]).astype(BF16)
    z = jnp.dot(a, w_ref[...], preferred_element_type=F32) + b_ref[...]
    half = z.shape[1] // 2
    u_out[...] = z[:, :half] * _sigmoid(z[:, half:])


def _glu(h3, mods, w, b, B, tpb_lat):
    D = D_MODEL
    return pl.pallas_call(
        _glu_kernel,
        grid=(B, tpb_lat),
        in_specs=[pl.BlockSpec((None, ROW_TILE, D), lambda b, j: (b, j + 1, 0)),
                  pl.BlockSpec((None, 6, D), lambda b, j: (b, 0, 0)),
                  pl.BlockSpec(w.shape, lambda b, j: (0, 0)),
                  pl.BlockSpec((1, w.shape[1]), lambda b, j: (0, 0))],
        out_specs=pl.BlockSpec((None, ROW_TILE, D), lambda b, j: (b, j, 0)),
        out_shape=jax.ShapeDtypeStruct((B, tpb_lat * ROW_TILE, D), F32),
        compiler_params=_params("arbitrary", "arbitrary"),
        name="conv_glu",
    )(h3, mods, w, b)


def _dwconv_kernel(prev_ref, cur_ref, next_ref, w_ref, b_ref, lg_ref, lb_ref, o_ref, win_ref, shift_ref, *,
                   tpb_lat):
    j = pl.program_id(1)
    H = CONV_HALO
    zero = jnp.zeros((H, cur_ref.shape[1]), F32)
    win_ref[0:H, :] = jnp.where(j > 0, prev_ref[...], zero)
    win_ref[H:H + ROW_TILE, :] = cur_ref[...]
    win_ref[H + ROW_TILE:, :] = jnp.where(j < tpb_lat - 1, next_ref[...], zero)
    w = w_ref[...]
    off = H - CONV_WIDTH // 2
    acc = jnp.zeros(cur_ref.shape, F32) + b_ref[...]
    sub = ROW_SHAPE[0]
    for r in range(sub):
        taps = [k for k in range(CONV_WIDTH) if (off + k) % sub == r]
        if not taps:
            continue
        src = win_ref
        if r:
            span = (off + taps[-1]) // sub * sub + ROW_TILE
            shift_ref[r % 2, 0:span, :] = win_ref[r:r + span, :]
            src = shift_ref.at[r % 2]
        for k in taps:
            a0 = (off + k) // sub * sub
            acc = acc + src[a0:a0 + ROW_TILE, :] * w[k:k + 1, :]
    y = _layernorm(acc, lg_ref[...], lb_ref[...])
    o_ref[...] = (y * _sigmoid(y)).astype(o_ref.dtype)


def _dwconv(u, w, b, ln_g, ln_b, B, tpb_lat):
    D = D_MODEL
    n = tpb_lat * ROW_TILE
    per = ROW_TILE // CONV_HALO
    last = n // CONV_HALO - 1
    const = lambda b_, j: (0, 0)
    return pl.pallas_call(
        functools.partial(_dwconv_kernel, tpb_lat=tpb_lat),
        grid=(B, tpb_lat),
        in_specs=[pl.BlockSpec((None, CONV_HALO, D), lambda b_, j: (b_, jnp.maximum(j * per - 1, 0), 0)),
                  pl.BlockSpec((None, ROW_TILE, D), lambda b_, j: (b_, j, 0)),
                  pl.BlockSpec((None, CONV_HALO, D), lambda b_, j: (b_, jnp.minimum((j + 1) * per, last), 0)),
                  pl.BlockSpec(w.shape, const),
                  pl.BlockSpec((1, D), const),
                  pl.BlockSpec((1, D), const),
                  pl.BlockSpec((1, D), const)],
        out_specs=pl.BlockSpec((ROW_TILE, D), lambda b_, j: (b_ * tpb_lat + j, 0)),
        out_shape=jax.ShapeDtypeStruct((B * n, D), BF16),
        scratch_shapes=[pltpu.VMEM((ROW_TILE + 2 * CONV_HALO, D), F32),
                        pltpu.VMEM((2, ROW_TILE + 2 * CONV_HALO, D), F32)],
        compiler_params=_params("arbitrary", "arbitrary"),
        name="conv_dw",
    )(u, u, u, w, b, ln_g, ln_b)


def kernel(x, c, ctx, c_ctx, mod_w, mod_b, ln1_g, ln1_b, ln2_g, ln2_b, mix_w_in, mla_q_norm, mla_kv_norm, mla_w_uq, mla_w_uk, mla_w_uv, hg_lb_logits, hg_norm, mix_w_out, conv_pw1_w, conv_pw1_b, conv_dw_w, conv_dw_b, conv_ln_g, conv_ln_b, conv_pw2_w, conv_pw2_b, router_w, router_b, exp_w_gu, exp_b_gu, exp_w_down, exp_b_down):
    B, n, D = x.shape
    tc = ctx.shape[1]
    assert D == D_MODEL and tc == ROW_TILE and n % ROW_TILE == 0 and n % GRID_W == 0
    assert B + 1 <= 16
    T = tc + n
    tpb = T // ROW_TILE
    tpb_lat = n // ROW_TILE
    N = B * T
    row2 = lambda v: v.reshape(1, -1)

    cond = jnp.zeros((16, D), F32).at[:B].set(c).at[B].set(c_ctx)
    mods = _modulation(cond, mod_w, mod_b)[:, :B + 1].reshape(DEPTH, B + 1, 6, D)
    lb_all = jnp.cumsum(jax.nn.softmax(hg_lb_logits.astype(F32), axis=0), axis=0)

    win, wq2, wkv = _inproj_weights(mix_w_in[0], mla_w_uq[0], mla_w_uk[0], mla_w_uv[0])
    cos_t, sin_t = _rope_tables(n, tc)
    q, k, v, hg = _inproj(ctx, x, mods[0], win, row2(mla_q_norm[0]), row2(mla_kv_norm[0]),
                          wq2, wkv, cos_t, sin_t, B, tpb)
    att = _attention(q, k, v, B, tpb, tc)
    hgo = _hgrn(hg, lb_all[0].reshape(HG_HEADS, 1, HG_DK), row2(hg_norm[0]), B, tpb)
    w_out = mix_w_out[0].astype(BF16)
    n_att = MLA_HEADS * MLA_V
    rw, rb = _router_weights(router_w[0], router_b[0])
    mod_map0 = lambda i: (jnp.where(i % tpb == 0, B, i // tpb), 0, 0)
    h1, b_in, *plan = _post(
        [att, hgo], [w_out[:n_att], w_out[n_att:]], jnp.zeros((1, D), F32),
        [ctx, x], [_ctx_map(tpb), _lat_map(tpb)], tpb, mods[0], mod_map0,
        row2(ln1_g[0]), row2(ln1_b[0]), rw, rb, N)
    ew = (exp_w_gu, exp_b_gu, exp_w_down, exp_b_down)
    h2 = _moe(b_in, plan, h1, mods[0], mod_map0, row2(ln2_g[0]), row2(ln2_b[0]), 0, *ew)

    n_lat = B * n
    h2_3 = h2.reshape(B * tpb, ROW_TILE, D)
    u = _glu(h2.reshape(B, T, D), mods[1], conv_pw1_w[0].astype(BF16), row2(conv_pw1_b[0]), B, tpb_lat)
    cv = _dwconv(u, conv_dw_w[0], row2(conv_dw_b[0]), row2(conv_ln_g[0]), row2(conv_ln_b[0]), B, tpb_lat)
    rw, rb = _router_weights(router_w[1], router_b[1])
    mod_map1 = lambda i: (i // tpb_lat, 0, 0)
    h_map1 = lambda i: ((i // tpb_lat) * tpb + i % tpb_lat + 1, 0, 0)
    h3, b_in, *plan = _post(
        [cv], [conv_pw2_w[0].astype(BF16)], row2(conv_pw2_b[0]),
        [h2_3], [h_map1], None, mods[1], mod_map1, row2(ln1_g[1]), row2(ln1_b[1]), rw, rb, n_lat)
    h4 = _moe(b_in, plan, h3, mods[1], mod_map1, row2(ln2_g[1]), row2(ln2_b[1]), 1, *ew)
    return h4.reshape(B, n, D)
```
